```python
import math
import jax, jax.numpy as jnp
from jax import lax
import numpy as np

D_MODEL = 2048
BATCH = 2
SEQ = 4096
DEPTH = 2

POOL_GROUPS = 4
POOL_WINDOWS = (2, 4, 8, 16)
POOL_WIDTH = D_MODEL // 2
POOL_GROUP_DIM = POOL_WIDTH // POOL_GROUPS
SB_HEADS = 8
SB_HEAD_DIM = 128
SB_WIDTH = SB_HEADS * SB_HEAD_DIM
Q_BLOCK = 128
N_BRANCHES = 2
IN_COLS = POOL_WIDTH + 3 * SB_WIDTH + N_BRANCHES * D_MODEL
D_FF = ((8 * D_MODEL // 3 + 255) // 256) * 256
N_EXPERTS = 8
TOP_K = 2
N_DENSE = (DEPTH + 1) // 2
N_MOE = DEPTH // 2
RMS_EPS = 1e-6

kernel_name = "hybrid_pool_stickbreak_gated_moe"


def rms_norm(x, g):
    xf = x.astype(jnp.float32)
    y = xf * lax.rsqrt(jnp.mean(xf * xf, axis=-1, keepdims=True) + RMS_EPS)
    return (y * g.astype(jnp.float32)).astype(x.dtype)


def causal_multiscale_pool(u, pool_w, pool_scale):
    B, S, _ = u.shape
    uf = u.astype(jnp.float32)
    cs = jnp.pad(jnp.cumsum(uf, axis=1), ((0, 0), (1, 0), (0, 0)))
    pos = jnp.arange(S, dtype=jnp.float32) + 1.0
    group_outs = []
    for gi, w in enumerate(POOL_WINDOWS):
        c0, c1 = gi * POOL_GROUP_DIM, (gi + 1) * POOL_GROUP_DIM
        csg = cs[:, :, c0:c1]
        hi = csg[:, 1:]
        lo = jnp.pad(csg[:, :S + 1 - w], ((0, 0), (w - 1, 0), (0, 0)))
        count = jnp.minimum(pos, float(w))[None, :, None]
        group_outs.append((hi - lo) / count - uf[:, :, c0:c1])
    pooled = jnp.stack(group_outs, axis=2)
    mixed = jnp.einsum('bsgc,gcd->bsgd', pooled, pool_w.astype(jnp.float32))
    out = mixed.reshape(B, S, POOL_WIDTH) * pool_scale.astype(jnp.float32)
    return out.astype(u.dtype)


def stick_breaking_attention(q, k, v):
    S = q.shape[2]
    scale = 1.0 / math.sqrt(SB_HEAD_DIM)
    outs = []
    for start in range(0, S, Q_BLOCK):
        end = start + Q_BLOCK
        qb = q[:, :, start:end]
        kb = k[:, :, :end]
        vb = v[:, :, :end]
        z = jnp.einsum('bhqd,bhkd->bhqk', qb, kb,
                       preferred_element_type=jnp.float32) * scale
        q_pos = start + jnp.arange(Q_BLOCK)[:, None]
        k_pos = jnp.arange(end)[None, :]
        causal = k_pos < q_pos
        log_not = jnp.where(causal, jax.nn.log_sigmoid(-z), 0.0)
        suffix = lax.cumsum(log_not, axis=3, reverse=True)
        log_a = jnp.where(causal, z + suffix, -jnp.inf)
        a = jnp.exp(log_a)
        outs.append(jnp.einsum('bhqk,bhkd->bhqd', a.astype(v.dtype), vb))
    return jnp.concatenate(outs, axis=2)


def hybrid_mixer(h, w_in, pool_w, pool_scale, w_pool_up, w_attn_up, w_out):
    B, S, _ = h.shape
    proj = h @ w_in
    splits = [POOL_WIDTH, POOL_WIDTH + SB_WIDTH, POOL_WIDTH + 2 * SB_WIDTH,
              POOL_WIDTH + 3 * SB_WIDTH]
    u_pool, q, k, v, gate_logits = jnp.split(proj, splits, axis=-1)
    y_pool = causal_multiscale_pool(u_pool, pool_w, pool_scale)
    to_heads = lambda t: t.reshape(B, S, SB_HEADS, SB_HEAD_DIM).transpose(0, 2, 1, 3)
    y_attn = stick_breaking_attention(to_heads(q), to_heads(k), to_heads(v))
    y_attn = y_attn.transpose(0, 2, 1, 3).reshape(B, S, SB_WIDTH)
    gates = jax.nn.sigmoid(gate_logits.reshape(B, S, N_BRANCHES, D_MODEL))
    merged = gates[:, :, 0] * (y_pool @ w_pool_up) + gates[:, :, 1] * (y_attn @ w_attn_up)
    return merged @ w_out


def swiglu(h, w_gate, w_up, w_down):
    return (jax.nn.silu(h @ w_gate) * (h @ w_up)) @ w_down


def moe_swiglu(h, w_router, we_gate, we_up, we_down):
    B, S, D = h.shape
    t = h.reshape(B * S, D)
    logits = (t @ w_router).astype(jnp.float32)
    top_vals, top_idx = lax.top_k(logits, TOP_K)
    top_w = jax.nn.softmax(top_vals, axis=-1)
    combine = jnp.sum(jax.nn.one_hot(top_idx, N_EXPERTS, dtype=jnp.float32)
                      * top_w[..., None], axis=1)
    out = jnp.zeros_like(t)
    for e in range(N_EXPERTS):
        ye = swiglu(t, we_gate[e], we_up[e], we_down[e])
        out = out + combine[:, e:e + 1].astype(t.dtype) * ye
    return out.reshape(B, S, D)


def setup_inputs(seed: int = 0) -> dict:
    key = jax.random.key(seed)
    ks = jax.random.split(key, 20)
    nrm = lambda k, shape, fan_in: jax.random.normal(k, shape, jnp.float32) * (fan_in ** -0.5)
    gain = lambda k, shape: 1.0 + 0.02 * jax.random.normal(k, shape, jnp.float32)
    return {
        "x": jax.random.normal(ks[0], (BATCH, SEQ, D_MODEL), jnp.float32),
        "norm_mix": gain(ks[1], (DEPTH, D_MODEL)),
        "w_in": nrm(ks[2], (DEPTH, D_MODEL, IN_COLS), D_MODEL),
        "pool_w": nrm(ks[3], (DEPTH, POOL_GROUPS, POOL_GROUP_DIM, POOL_GROUP_DIM), POOL_GROUP_DIM),
        "pool_scale": gain(ks[4], (DEPTH, POOL_WIDTH)),
        "w_pool_up": nrm(ks[5], (DEPTH, POOL_WIDTH, D_MODEL), POOL_WIDTH),
        "w_attn_up": nrm(ks[6], (DEPTH, SB_WIDTH, D_MODEL), SB_WIDTH),
        "w_out": nrm(ks[7], (DEPTH, D_MODEL, D_MODEL), D_MODEL),
        "norm_ffn": gain(ks[8], (DEPTH, D_MODEL)),
        "ffn_gate": nrm(ks[9], (N_DENSE, D_MODEL, D_FF), D_MODEL),
        "ffn_up": nrm(ks[10], (N_DENSE, D_MODEL, D_FF), D_MODEL),
        "ffn_down": nrm(ks[11], (N_DENSE, D_FF, D_MODEL), D_FF),
        "w_router": nrm(ks[12], (N_MOE, D_MODEL, N_EXPERTS), D_MODEL),
        "moe_gate": nrm(ks[13], (N_MOE, N_EXPERTS, D_MODEL, D_FF), D_MODEL),
        "moe_up": nrm(ks[14], (N_MOE, N_EXPERTS, D_MODEL, D_FF), D_MODEL),
        "moe_down": nrm(ks[15], (N_MOE, N_EXPERTS, D_FF, D_MODEL), D_FF),
        "norm_final": gain(ks[16], (D_MODEL,)),
    }


def reference(x, norm_mix, w_in, pool_w, pool_scale, w_pool_up, w_attn_up, w_out,
              norm_ffn, ffn_gate, ffn_up, ffn_down, w_router, moe_gate, moe_up,
              moe_down, norm_final):
    for i in range(DEPTH):
        h = rms_norm(x, norm_mix[i])
        x = x + hybrid_mixer(h, w_in[i], pool_w[i], pool_scale[i],
                             w_pool_up[i], w_attn_up[i], w_out[i])
        h = rms_norm(x, norm_ffn[i])
        j = i // 2
        if i % 2 == 0:
            x = x + swiglu(h, ffn_gate[j], ffn_up[j], ffn_down[j])
        else:
            x = x + moe_swiglu(h, w_router[j], moe_gate[j], moe_up[j], moe_down[j])
    return rms_norm(x, norm_final)
```

```python
import functools
import math

import jax
import jax.numpy as jnp
from jax import lax
from jax.experimental import pallas as pl
from jax.experimental.pallas import tpu as pltpu

F32 = jnp.float32
BF16 = jnp.bfloat16

RMS_EPS = 1e-6
POOL_WINDOWS = (2, 4, 8, 16)
POOL_HALO = 16
SB_HEADS = 8
SB_HEAD_DIM = 128
N_EXPERTS = 8
TOP_K = 2

VMEM_LIMIT_BYTES = 56 * 1024 * 1024

IN_TM, IN_TN = 1024, 512
ATT_T = 256
MIX_TM = 256
FFN_TM, FFN_FC = 512, 512
RT_TM = 512
MOE_TM, MOE_FC = 512, 512
CMB_TM = 512
NORM_CHUNK = 128


def _params(sem):
    return pltpu.CompilerParams(dimension_semantics=sem, vmem_limit_bytes=VMEM_LIMIT_BYTES)


def _rms_rows(x, g):
    ms = jnp.mean(x * x, axis=-1, keepdims=True)
    return x * lax.rsqrt(ms + RMS_EPS) * g


def _rms_to_scratch(x_ref, g_ref, h_ref, rows):
    g = g_ref[...]

    def body(c, carry):
        r0 = pl.multiple_of(c * NORM_CHUNK, NORM_CHUNK)
        h_ref[pl.ds(r0, NORM_CHUNK), :] = _rms_rows(x_ref[pl.ds(r0, NORM_CHUNK), :], g).astype(h_ref.dtype)
        return carry

    lax.fori_loop(0, rows // NORM_CHUNK, body, 0)


def _in_proj_kernel(x_ref, g_ref, w_ref, u_ref, qkv_ref, gate_ref, h_ref, *, n_pool, n_q, n_qkv, q_scale):
    j = pl.program_id(1)

    @pl.when(j == 0)
    def _():
        _rms_to_scratch(x_ref, g_ref, h_ref, x_ref.shape[0])

    acc = jnp.dot(h_ref[...], w_ref[...], preferred_element_type=F32)

    @pl.when(j < n_pool)
    def _():
        u_ref[...] = acc

    @pl.when((j >= n_pool) & (j < n_pool + n_q))
    def _():
        qkv_ref[...] = (acc * q_scale).astype(BF16)

    @pl.when((j >= n_pool + n_q) & (j < n_pool + n_qkv))
    def _():
        qkv_ref[...] = acc.astype(BF16)

    @pl.when(j >= n_pool + n_qkv)
    def _():
        gate_ref[...] = 1.0 / (1.0 + jnp.exp(-acc))


def _in_proj(x, g, w, pool_w, sb_w):
    n, d = x.shape
    cols = w.shape[1]
    gate_w = cols - pool_w - 3 * sb_w
    tm, tn = IN_TM, IN_TN
    n_pool, n_q, n_qkv, n_gate = pool_w // tn, sb_w // tn, 3 * sb_w // tn, gate_w // tn
    kern = functools.partial(_in_proj_kernel, n_pool=n_pool, n_q=n_q, n_qkv=n_qkv,
                             q_scale=1.0 / math.sqrt(SB_HEAD_DIM))
    return pl.pallas_call(
        kern,
        grid=(n // tm, cols // tn),
        in_specs=[
            pl.BlockSpec((tm, d), lambda i, j: (i, 0)),
            pl.BlockSpec((1, d), lambda i, j: (0, 0)),
            pl.BlockSpec((d, tn), lambda i, j: (0, j)),
        ],
        out_specs=[
            pl.BlockSpec((tm, tn), lambda i, j: (i, jnp.minimum(j, n_pool - 1))),
            pl.BlockSpec((tm, tn), lambda i, j: (i, jnp.clip(j - n_pool, 0, n_qkv - 1))),
            pl.BlockSpec((tm, tn), lambda i, j: (i, jnp.maximum(j - n_pool - n_qkv, 0))),
        ],
        out_shape=[
            jax.ShapeDtypeStruct((n, pool_w), F32),
            jax.ShapeDtypeStruct((n, 3 * sb_w), BF16),
            jax.ShapeDtypeStruct((n, gate_w), F32),
        ],
        scratch_shapes=[pltpu.VMEM((tm, d), BF16)],
        compiler_params=_params(("arbitrary", "arbitrary")),
        name="in_proj",
    )(x, g, w)


def _attn_kernel(q_ref, k_ref, v_ref, o_ref):
    t = ATT_T
    qi = pl.program_id(2)
    q = q_ref[...]

    row = lax.broadcasted_iota(jnp.int32, (t, t), 0)
    col = lax.broadcasted_iota(jnp.int32, (t, t), 1)
    tri = (row >= col).astype(BF16)
    causal = col < row

    def scores(kb):
        k0 = pl.multiple_of(kb * t, t)
        kblk = k_ref[pl.ds(k0, t), :]
        vblk = v_ref[pl.ds(k0, t), :]
        z = lax.dot_general(q, kblk, (((1,), (1,)), ((), ())), preferred_element_type=F32)
        log_not = -(jnp.maximum(z, 0.0) + jnp.log1p(jnp.exp(-jnp.abs(z))))
        return z, log_not, vblk

    def suffix_sum(log_not):
        hi = log_not.astype(BF16)
        lo = (log_not - hi.astype(F32)).astype(BF16)
        return (jnp.dot(hi, tri, preferred_element_type=F32)
                + jnp.dot(lo, tri, preferred_element_type=F32))

    z, log_not, vblk = scores(qi)
    log_not = jnp.where(causal, log_not, 0.0)
    suf = suffix_sum(log_not)
    a = jnp.where(causal, jnp.exp(z + suf), 0.0)
    acc = jnp.dot(a.astype(BF16), vblk, preferred_element_type=F32)
    carry = suf[:, 0:1]

    def body(it, state):
        acc, carry = state
        z, log_not, vblk = scores(qi - 1 - it)
        suf = suffix_sum(log_not)
        a = jnp.exp(z + suf + carry)
        acc = acc + jnp.dot(a.astype(BF16), vblk, preferred_element_type=F32)
        return acc, carry + suf[:, 0:1]

    acc, _ = lax.fori_loop(0, qi, body, (acc, carry))
    o_ref[...] = acc.astype(o_ref.dtype)


def _attention(qkv, batch, seq):
    n = qkv.shape[0]
    t, h, dh = ATT_T, SB_HEADS, SB_HEAD_DIM
    nq = seq // t
    return pl.pallas_call(
        _attn_kernel,
        grid=(batch, h, nq),
        in_specs=[
            pl.BlockSpec((t, dh), lambda b, hh, i: (b * nq + i, hh)),
            pl.BlockSpec((seq, dh), lambda b, hh, i: (b, h + hh)),
            pl.BlockSpec((seq, dh), lambda b, hh, i: (b, 2 * h + hh)),
        ],
        out_specs=pl.BlockSpec((t, dh), lambda b, hh, i: (b * nq + i, hh)),
        out_shape=jax.ShapeDtypeStruct((n, h * dh), BF16),
        compiler_params=_params(("arbitrary", "arbitrary", "arbitrary")),
        name="sb_attention",
    )(qkv, qkv, qkv)


def _mix_out_kernel(u_ref, up_ref, ya_ref, gate_ref, x_ref, pw_ref, ps_ref, wpu_ref, wau_ref, wo_ref,
                    o_ref, *, seq):
    tm, pool_w = u_ref.shape
    d = x_ref.shape[1]
    gd = pool_w // len(POOL_WINDOWS)
    i = pl.program_id(0)
    row0 = (i * tm) % seq

    u = u_ref[...]
    halo = jnp.where(row0 > 0, up_ref[...], 0.0)
    pos = (row0 + lax.broadcasted_iota(jnp.int32, (tm, 1), 0) + 1).astype(F32)

    mixed = []
    for gi, w in enumerate(POOL_WINDOWS):
        c0 = gi * gd
        s = jnp.concatenate([halo[:, c0:c0 + gd], u[:, c0:c0 + gd]], axis=0)
        step = 1
        while step < w:
            s = s + pltpu.roll(s, step, axis=0)
            step *= 2
        pooled = s[POOL_HALO:, :] / jnp.minimum(pos, float(w)) - u[:, c0:c0 + gd]
        mixed.append(jnp.dot(pooled.astype(BF16), pw_ref[gi], preferred_element_type=F32))
    y_pool = (jnp.concatenate(mixed, axis=1) * ps_ref[...]).astype(BF16)

    p_up = jnp.dot(y_pool, wpu_ref[...], preferred_element_type=F32)
    a_up = jnp.dot(ya_ref[...], wau_ref[...], preferred_element_type=F32)
    merged = gate_ref[:, :d] * p_up + gate_ref[:, d:] * a_up
    o_ref[...] = x_ref[...] + jnp.dot(merged.astype(BF16), wo_ref[...], preferred_element_type=F32)


def _const_spec(shape):
    nd = len(shape)
    return pl.BlockSpec(shape, lambda i: (0,) * nd, pipeline_mode=pl.Buffered(1))


def _mix_out(u, y_attn, gates, x, pool_w, pool_scale, w_pool_up, w_attn_up, w_out, seq):
    n, d = x.shape
    pw = u.shape[1]
    tm = MIX_TM
    hb = tm // POOL_HALO
    return pl.pallas_call(
        functools.partial(_mix_out_kernel, seq=seq),
        grid=(n // tm,),
        in_specs=[
            pl.BlockSpec((tm, pw), lambda i: (i, 0)),
            pl.BlockSpec((POOL_HALO, pw), lambda i: (jnp.maximum(i * hb - 1, 0), 0)),
            pl.BlockSpec((tm, y_attn.shape[1]), lambda i: (i, 0)),
            pl.BlockSpec((tm, 2 * d), lambda i: (i, 0)),
            pl.BlockSpec((tm, d), lambda i: (i, 0)),
            _const_spec(pool_w.shape),
            _const_spec(pool_scale.shape),
            _const_spec(w_pool_up.shape),
            _const_spec(w_attn_up.shape),
            _const_spec(w_out.shape),
        ],
        out_specs=pl.BlockSpec((tm, d), lambda i: (i, 0)),
        out_shape=jax.ShapeDtypeStruct((n, d), F32),
        compiler_params=_params(("arbitrary",)),
        name="mix_out",
    )(u, u, y_attn, gates, x, pool_w, pool_scale, w_pool_up, w_attn_up, w_out)


def _silu(x):
    return x / (1.0 + jnp.exp(-x))


def _ffn_kernel(x_ref, g_ref, wg_ref, wu_ref, wd_ref, gf_ref, o_ref, h_ref, *, final_norm):
    j = pl.program_id(1)

    @pl.when(j == 0)
    def _():
        _rms_to_scratch(x_ref, g_ref, h_ref, x_ref.shape[0])
        o_ref[...] = x_ref[...]

    h = h_ref[...]
    act = _silu(jnp.dot(h, wg_ref[...], preferred_element_type=F32)) * jnp.dot(
        h, wu_ref[...], preferred_element_type=F32)
    o_ref[...] += jnp.dot(act.astype(BF16), wd_ref[...], preferred_element_type=F32)

    if final_norm:
        @pl.when(j == pl.num_programs(1) - 1)
        def _():
            o_ref[...] = _rms_rows(o_ref[...], gf_ref[...])


def _ffn(x, g, w_gate, w_up, w_down, g_final, final_norm):
    n, d = x.shape
    dff = w_gate.shape[1]
    tm, fc = FFN_TM, FFN_FC
    return pl.pallas_call(
        functools.partial(_ffn_kernel, final_norm=final_norm),
        grid=(n // tm, dff // fc),
        in_specs=[
            pl.BlockSpec((tm, d), lambda i, j: (i, 0)),
            pl.BlockSpec((1, d), lambda i, j: (0, 0)),
            pl.BlockSpec((d, fc), lambda i, j: (0, j)),
            pl.BlockSpec((d, fc), lambda i, j: (0, j)),
            pl.BlockSpec((fc, d), lambda i, j: (j, 0)),
            pl.BlockSpec((1, d), lambda i, j: (0, 0)),
        ],
        out_specs=pl.BlockSpec((tm, d), lambda i, j: (i, 0)),
        out_shape=jax.ShapeDtypeStruct((n, d), F32),
        scratch_shapes=[pltpu.VMEM((tm, d), BF16)],
        compiler_params=_params(("arbitrary", "arbitrary")),
        name="ffn_dense",
    )(x, g, w_gate, w_up, w_down, g_final)


def _router_kernel(x_ref, g_ref, wr_ref, h_ref, idx_ref, wt_ref):
    h = _rms_rows(x_ref[...], g_ref[...])
    h_hi = h.astype(BF16)
    h_ref[...] = h_hi
    h_lo = (h - h_hi.astype(F32)).astype(BF16)
    wr = wr_ref[...]
    w_hi = wr.astype(BF16)
    w_lo = (wr - w_hi.astype(F32)).astype(BF16)
    nt = (((1,), (1,)), ((), ()))
    logits = (lax.dot_general(w_hi, h_hi, nt, preferred_element_type=F32)
              + lax.dot_general(w_lo, h_hi, nt, preferred_element_type=F32)
              + lax.dot_general(w_hi, h_lo, nt, preferred_element_type=F32))

    ne = logits.shape[0]
    eid = lax.broadcasted_iota(jnp.int32, logits.shape, 0)
    m1 = jnp.max(logits, axis=0, keepdims=True)
    i1 = jnp.min(jnp.where(logits == m1, eid, ne), axis=0, keepdims=True)
    rest = jnp.where(eid == i1, -jnp.inf, logits)
    m2 = jnp.max(rest, axis=0, keepdims=True)
    i2 = jnp.min(jnp.where(rest == m2, eid, ne), axis=0, keepdims=True)
    e = jnp.exp(m2 - m1)
    idx_ref[0:1, :] = i1
    idx_ref[1:2, :] = i2
    wt_ref[0:1, :] = 1.0 / (1.0 + e)
    wt_ref[1:2, :] = e / (1.0 + e)


def _router(x, g, w_router_t):
    n, d = x.shape
    tm = RT_TM
    return pl.pallas_call(
        _router_kernel,
        grid=(n // tm,),
        in_specs=[
            pl.BlockSpec((tm, d), lambda i: (i, 0)),
            pl.BlockSpec((1, d), lambda i: (0, 0)),
            pl.BlockSpec(w_router_t.shape, lambda i: (0, 0)),
        ],
        out_specs=[
            pl.BlockSpec((tm, d), lambda i: (i, 0)),
            pl.BlockSpec((TOP_K, tm), lambda i: (0, i)),
            pl.BlockSpec((TOP_K, tm), lambda i: (0, i)),
        ],
        out_shape=[
            jax.ShapeDtypeStruct((n, d), BF16),
            jax.ShapeDtypeStruct((TOP_K, n), jnp.int32),
            jax.ShapeDtypeStruct((TOP_K, n), F32),
        ],
        compiler_params=_params(("arbitrary",)),
        name="moe_router",
    )(x, g, w_router_t)


def _moe_kernel(te_ref, tv_ref, hs_ref, ws_ref, wg_ref, wu_ref, wd_ref, y_ref):
    t = pl.program_id(0)
    j = pl.program_id(1)
    valid = tv_ref[t] > 0

    @pl.when(j == 0)
    def _():
        y_ref[...] = jnp.zeros_like(y_ref)

    @pl.when(valid)
    def _():
        h = hs_ref[...]
        act = _silu(jnp.dot(h, wg_ref[...], preferred_element_type=F32)) * jnp.dot(
            h, wu_ref[...], preferred_element_type=F32)
        y_ref[...] += jnp.dot(act.astype(BF16), wd_ref[...], preferred_element_type=F32)

    @pl.when(valid & (j == pl.num_programs(1) - 1))
    def _():
        y_ref[...] = y_ref[...] * ws_ref[...]


def _moe_gmm(tile_expert, tile_valid, hs, slot_w, w_gate, w_up, w_down):
    p, d = hs.shape
    dff = w_gate.shape[2]
    tm, fc = MOE_TM, MOE_FC
    nfc = dff // fc

    def chunk(t, j, tv):
        return jnp.where(tv[t] > 0, j, nfc - 1)

    grid_spec = pltpu.PrefetchScalarGridSpec(
        num_scalar_prefetch=2,
        grid=(p // tm, nfc),
        in_specs=[
            pl.BlockSpec((tm, d), lambda t, j, te, tv: (t, 0)),
            pl.BlockSpec((tm, 1), lambda t, j, te, tv: (t, 0)),
            pl.BlockSpec((None, d, fc), lambda t, j, te, tv: (te[t], 0, chunk(t, j, tv))),
            pl.BlockSpec((None, d, fc), lambda t, j, te, tv: (te[t], 0, chunk(t, j, tv))),
            pl.BlockSpec((None, fc, d), lambda t, j, te, tv: (te[t], chunk(t, j, tv), 0)),
        ],
        out_specs=pl.BlockSpec((tm, d), lambda t, j, te, tv: (t, 0)),
    )
    return pl.pallas_call(
        _moe_kernel,
        grid_spec=grid_spec,
        out_shape=jax.ShapeDtypeStruct((p, d), F32),
        compiler_params=_params(("arbitrary", "arbitrary")),
        name="moe_gmm",
    )(tile_expert, tile_valid, hs, slot_w, w_gate, w_up, w_down)


def _moe_plan(idx, n_tokens):
    tm = MOE_TM
    n_assign = TOP_K * n_tokens
    n_slots = n_assign + N_EXPERTS * tm
    expert = idx.reshape(-1)
    onehot = (expert[:, None] == jnp.arange(N_EXPERTS, dtype=jnp.int32)[None, :]).astype(jnp.int32)
    ranks = jnp.cumsum(onehot, axis=0) - onehot
    counts = jnp.sum(onehot, axis=0)
    padded = ((counts + tm - 1) // tm) * tm
    ends = jnp.cumsum(padded)
    offsets = ends - padded
    slot = offsets[expert] + jnp.sum(ranks * onehot, axis=1)
    token = jnp.arange(n_assign, dtype=jnp.int32) % n_tokens
    token_of_slot = jnp.zeros((n_slots,), jnp.int32).at[slot].set(token)
    tile_start = jnp.arange(n_slots // tm, dtype=jnp.int32) * tm
    tile_valid = (tile_start < ends[-1]).astype(jnp.int32)
    tile_expert = jnp.minimum(jnp.searchsorted(ends, tile_start, side="right"), N_EXPERTS - 1).astype(jnp.int32)
    last_expert = tile_expert[jnp.maximum(ends[-1] // tm - 1, 0)]
    tile_expert = jnp.where(tile_valid > 0, tile_expert, last_expert)
    return slot, token_of_slot, tile_expert, tile_valid, n_slots


def _combine_kernel(x_ref, y1_ref, y2_ref, gf_ref, o_ref, *, final_norm):
    out = x_ref[...] + y1_ref[...] + y2_ref[...]
    if final_norm:
        out = _rms_rows(out, gf_ref[...])
    o_ref[...] = out


def _combine(x, y1, y2, g_final, final_norm):
    n, d = x.shape
    tm = CMB_TM
    row = pl.BlockSpec((tm, d), lambda i: (i, 0))
    return pl.pallas_call(
        functools.partial(_combine_kernel, final_norm=final_norm),
        grid=(n // tm,),
        in_specs=[row, row, row, pl.BlockSpec((1, d), lambda i: (0, 0))],
        out_specs=row,
        out_shape=jax.ShapeDtypeStruct((n, d), F32),
        compiler_params=_params(("arbitrary",)),
        name="moe_combine",
    )(x, y1, y2, g_final)


def _moe_layer(x, g, w_router, w_gate, w_up, w_down, g_final, final_norm):
    n, d = x.shape
    h, idx, wts = _router(x, g, w_router.T)
    slot, token_of_slot, tile_expert, tile_valid, n_slots = _moe_plan(idx, n)
    hs = jnp.take(h, token_of_slot, axis=0)
    slot_w = jnp.zeros((n_slots,), F32).at[slot].set(wts.reshape(-1)).reshape(n_slots, 1)
    y = _moe_gmm(tile_expert, tile_valid, hs, slot_w, w_gate, w_up, w_down)
    y1 = jnp.take(y, slot[:n], axis=0)
    y2 = jnp.take(y, slot[n:], axis=0)
    return _combine(x, y1, y2, g_final, final_norm)


def kernel(x, norm_mix, w_in, pool_w, pool_scale, w_pool_up, w_attn_up, w_out, norm_ffn, ffn_gate,
           ffn_up, ffn_down, w_router, moe_gate, moe_up, moe_down, norm_final):
    batch, seq, d = x.shape
    depth = norm_mix.shape[0]
    pool_width = pool_scale.shape[1]
    sb_width = w_attn_up.shape[1]
    n = batch * seq
    xf = x.reshape(n, d)
    g_final = norm_final.reshape(1, d)

    for i in range(depth):
        last = i == depth - 1
        u, qkv, gates = _in_proj(xf, norm_mix[i].reshape(1, d), w_in[i].astype(BF16), pool_width, sb_width)
        y_attn = _attention(qkv, batch, seq)
        xf = _mix_out(u, y_attn, gates, xf, pool_w[i].astype(BF16), pool_scale[i].reshape(1, pool_width),
                      w_pool_up[i].astype(BF16), w_attn_up[i].astype(BF16), w_out[i].astype(BF16), seq)
        gf = norm_ffn[i].reshape(1, d)
        j = i // 2
        if i % 2 == 0:
            xf = _ffn(xf, gf, ffn_gate[j].astype(BF16), ffn_up[j].astype(BF16), ffn_down[j].astype(BF16),
                      g_final, last)
        else:
            xf = _moe_layer(xf, gf, w_router[j], moe_gate[j].astype(BF16), moe_up[j].astype(BF16),
                            moe_down[j].astype(BF16), g_final, last)
    return xf.reshape(batch, seq, d)
```

```python
import functools
import math

import jax
import jax.numpy as jnp
from jax import lax
from jax.experimental import pallas as pl
from jax.experimental.pallas import tpu as pltpu

F32 = jnp.float32
BF16 = jnp.bfloat16

RMS_EPS = 1e-6
LOG2E = 1.0 / math.log(2.0)
POOL_WINDOWS = (2, 4, 8, 16)
POOL_HALO = 16
SB_HEADS = 8
SB_HEAD_DIM = 128
N_EXPERTS = 8
TOP_K = 2

VMEM_LIMIT_BYTES = 56 * 1024 * 1024

IN_TM, IN_TN = 1024, 512
ATT_T = 256
ATT_HEADS = 4
MIX_TM = 256
FFN_TM, FFN_FC = 512, 512
RT_TM = 512
MOE_TM, MOE_FC = 512, 512
CMB_TM = 512
NORM_CHUNK = 128


def _params(sem):
    return pltpu.CompilerParams(dimension_semantics=sem, vmem_limit_bytes=VMEM_LIMIT_BYTES)


def _rms_rows(x, g):
    ms = jnp.mean(x * x, axis=-1, keepdims=True)
    return x * lax.rsqrt(ms + RMS_EPS) * g


def _rms_to_scratch(x_ref, g_ref, h_ref, rows):
    g = g_ref[...]

    def body(c, carry):
        r0 = pl.multiple_of(c * NORM_CHUNK, NORM_CHUNK)
        h_ref[pl.ds(r0, NORM_CHUNK), :] = _rms_rows(x_ref[pl.ds(r0, NORM_CHUNK), :], g).astype(h_ref.dtype)
        return carry

    lax.fori_loop(0, rows // NORM_CHUNK, body, 0)


def _in_proj_kernel(x_ref, g_ref, w_ref, u_ref, qkv_ref, gate_ref, h_ref, *, n_pool, n_q, n_qkv, q_scale):
    j = pl.program_id(1)

    @pl.when(j == 0)
    def _():
        _rms_to_scratch(x_ref, g_ref, h_ref, x_ref.shape[0])

    def project():
        return jnp.dot(h_ref[...], w_ref[...], preferred_element_type=F32)

    @pl.when(j < n_pool)
    def _():
        u_ref[...] = project()

    @pl.when((j >= n_pool) & (j < n_pool + n_q))
    def _():
        qkv_ref[...] = (project() * q_scale).astype(BF16)

    @pl.when((j >= n_pool + n_q) & (j < n_pool + n_qkv))
    def _():
        qkv_ref[...] = project().astype(BF16)

    @pl.when(j >= n_pool + n_qkv)
    def _():
        gate_ref[...] = 1.0 / (1.0 + jnp.exp(-project()))


def _in_proj(x, g, w, pool_w, sb_w):
    n, d = x.shape
    cols = w.shape[1]
    gate_w = cols - pool_w - 3 * sb_w
    tm, tn = IN_TM, IN_TN
    n_pool, n_q, n_qkv, n_gate = pool_w // tn, sb_w // tn, 3 * sb_w // tn, gate_w // tn
    kern = functools.partial(_in_proj_kernel, n_pool=n_pool, n_q=n_q, n_qkv=n_qkv,
                             q_scale=1.0 / math.sqrt(SB_HEAD_DIM))
    return pl.pallas_call(
        kern,
        grid=(n // tm, cols // tn),
        in_specs=[
            pl.BlockSpec((tm, d), lambda i, j: (i, 0)),
            pl.BlockSpec((1, d), lambda i, j: (0, 0)),
            pl.BlockSpec((d, tn), lambda i, j: (0, j)),
        ],
        out_specs=[
            pl.BlockSpec((tm, tn), lambda i, j: (i, jnp.minimum(j, n_pool - 1))),
            pl.BlockSpec((tm, tn), lambda i, j: (i, jnp.clip(j - n_pool, 0, n_qkv - 1))),
            pl.BlockSpec((tm, tn), lambda i, j: (i, jnp.maximum(j - n_pool - n_qkv, 0))),
        ],
        out_shape=[
            jax.ShapeDtypeStruct((n, pool_w), F32),
            jax.ShapeDtypeStruct((n, 3 * sb_w), BF16),
            jax.ShapeDtypeStruct((n, gate_w), F32),
        ],
        scratch_shapes=[pltpu.VMEM((tm, d), BF16)],
        compiler_params=_params(("arbitrary", "arbitrary")),
        name="in_proj",
    )(x, g, w)


def _attn_kernel(q_ref, k_ref, v_ref, o_ref, acc_ref):
    t, dh = ATT_T, SB_HEAD_DIM
    qi = pl.program_id(2)

    row = lax.broadcasted_iota(jnp.int32, (t, t), 0)
    col = lax.broadcasted_iota(jnp.int32, (t, t), 1)
    tri = (row >= col).astype(BF16)
    tri2 = jnp.concatenate([tri, tri], axis=0)
    causal = col < row

    heads = range(ATT_HEADS)
    cols = [slice(hh * dh, (hh + 1) * dh) for hh in heads]

    def block(kb, carries, diag):
        k0 = pl.multiple_of(kb * t, t)
        z = [lax.dot_general(q_ref[:, c], k_ref[pl.ds(k0, t), c], (((1,), (1,)), ((), ())),
                             preferred_element_type=F32) for c in cols]
        suf = []
        for hh in heads:
            sp = jnp.maximum(z[hh], 0.0) + jnp.log(1.0 + jnp.exp2(jnp.abs(z[hh]) * -LOG2E))
            if diag:
                sp = jnp.where(causal, sp, 0.0)
            hi = sp.astype(BF16)
            lo = (sp - hi.astype(F32)).astype(BF16)
            suf.append(jnp.dot(jnp.concatenate([hi, lo], axis=1), tri2, preferred_element_type=F32))
        new_carries = []
        for hh in heads:
            if diag:
                a = jnp.where(causal, jnp.exp(z[hh] - suf[hh]), 0.0)
            else:
                a = jnp.exp(z[hh] - suf[hh] - carries[hh])
            pv = jnp.dot(a.astype(BF16), v_ref[pl.ds(k0, t), cols[hh]], preferred_element_type=F32)
            if diag:
                acc_ref[hh] = pv
                new_carries.append(suf[hh][:, 0:1])
            else:
                acc_ref[hh] += pv
                new_carries.append(carries[hh] + suf[hh][:, 0:1])
        return tuple(new_carries)

    carries = block(qi, None, True)
    lax.fori_loop(0, qi, lambda it, c: block(qi - 1 - it, c, False), carries)
    for hh in heads:
        o_ref[:, cols[hh]] = acc_ref[hh].astype(o_ref.dtype)


def _attention(qkv, batch, seq):
    n = qkv.shape[0]
    t, h, dh = ATT_T, SB_HEADS, SB_HEAD_DIM
    nq = seq // t
    hg = h // ATT_HEADS
    w = ATT_HEADS * dh
    return pl.pallas_call(
        _attn_kernel,
        grid=(batch, hg, nq),
        in_specs=[
            pl.BlockSpec((t, w), lambda b, g, i: (b * nq + i, g)),
            pl.BlockSpec((seq, w), lambda b, g, i: (b, hg + g)),
            pl.BlockSpec((seq, w), lambda b, g, i: (b, 2 * hg + g)),
        ],
        out_specs=pl.BlockSpec((t, w), lambda b, g, i: (b * nq + i, g)),
        out_shape=jax.ShapeDtypeStruct((n, h * dh), BF16),
        scratch_shapes=[pltpu.VMEM((ATT_HEADS, t, dh), F32)],
        compiler_params=_params(("arbitrary", "arbitrary", "arbitrary")),
        name="sb_attention",
    )(qkv, qkv, qkv)


def _mix_out_kernel(u_ref, up_ref, ya_ref, gate_ref, x_ref, pw_ref, ps_ref, wpu_ref, wau_ref, wo_ref,
                    o_ref, *, seq):
    tm, pool_w = u_ref.shape
    d = x_ref.shape[1]
    gd = pool_w // len(POOL_WINDOWS)
    i = pl.program_id(0)
    row0 = (i * tm) % seq

    u = u_ref[...]
    halo = jnp.where(row0 > 0, up_ref[...], 0.0)
    pos = (row0 + lax.broadcasted_iota(jnp.int32, (tm, 1), 0) + 1).astype(F32)

    mixed = []
    for gi, w in enumerate(POOL_WINDOWS):
        c0 = gi * gd
        s = jnp.concatenate([halo[:, c0:c0 + gd], u[:, c0:c0 + gd]], axis=0)
        step = 1
        while step < w:
            s = s + pltpu.roll(s, step, axis=0)
            step *= 2
        pooled = s[POOL_HALO:, :] / jnp.minimum(pos, float(w)) - u[:, c0:c0 + gd]
        mixed.append(jnp.dot(pooled.astype(BF16), pw_ref[gi], preferred_element_type=F32))
    y_pool = (jnp.concatenate(mixed, axis=1) * ps_ref[...]).astype(BF16)

    p_up = jnp.dot(y_pool, wpu_ref[...], preferred_element_type=F32)
    a_up = jnp.dot(ya_ref[...], wau_ref[...], preferred_element_type=F32)
    merged = gate_ref[:, :d] * p_up + gate_ref[:, d:] * a_up
    o_ref[...] = x_ref[...] + jnp.dot(merged.astype(BF16), wo_ref[...], preferred_element_type=F32)


def _const_spec(shape):
    nd = len(shape)
    return pl.BlockSpec(shape, lambda i: (0,) * nd, pipeline_mode=pl.Buffered(1))


def _mix_out(u, y_attn, gates, x, pool_w, pool_scale, w_pool_up, w_attn_up, w_out, seq):
    n, d = x.shape
    pw = u.shape[1]
    tm = MIX_TM
    hb = tm // POOL_HALO
    return pl.pallas_call(
        functools.partial(_mix_out_kernel, seq=seq),
        grid=(n // tm,),
        in_specs=[
            pl.BlockSpec((tm, pw), lambda i: (i, 0)),
            pl.BlockSpec((POOL_HALO, pw), lambda i: (jnp.maximum(i * hb - 1, 0), 0)),
            pl.BlockSpec((tm, y_attn.shape[1]), lambda i: (i, 0)),
            pl.BlockSpec((tm, 2 * d), lambda i: (i, 0)),
            pl.BlockSpec((tm, d), lambda i: (i, 0)),
            _const_spec(pool_w.shape),
            _const_spec(pool_scale.shape),
            _const_spec(w_pool_up.shape),
            _const_spec(w_attn_up.shape),
            _const_spec(w_out.shape),
        ],
        out_specs=pl.BlockSpec((tm, d), lambda i: (i, 0)),
        out_shape=jax.ShapeDtypeStruct((n, d), F32),
        compiler_params=_params(("arbitrary",)),
        name="mix_out",
    )(u, u, y_attn, gates, x, pool_w, pool_scale, w_pool_up, w_attn_up, w_out)


def _silu(x):
    return x / (1.0 + jnp.exp(-x))


def _ffn_kernel(x_ref, g_ref, wg_ref, wu_ref, wd_ref, gf_ref, o_ref, h_ref, *, final_norm):
    j = pl.program_id(1)

    @pl.when(j == 0)
    def _():
        _rms_to_scratch(x_ref, g_ref, h_ref, x_ref.shape[0])
        o_ref[...] = x_ref[...]

    h = h_ref[...]
    act = _silu(jnp.dot(h, wg_ref[...], preferred_element_type=F32)) * jnp.dot(
        h, wu_ref[...], preferred_element_type=F32)
    o_ref[...] += jnp.dot(act.astype(BF16), wd_ref[...], preferred_element_type=F32)

    if final_norm:
        @pl.when(j == pl.num_programs(1) - 1)
        def _():
            o_ref[...] = _rms_rows(o_ref[...], gf_ref[...])


def _ffn(x, g, w_gate, w_up, w_down, g_final, final_norm):
    n, d = x.shape
    dff = w_gate.shape[1]
    tm, fc = FFN_TM, FFN_FC
    return pl.pallas_call(
        functools.partial(_ffn_kernel, final_norm=final_norm),
        grid=(n // tm, dff // fc),
        in_specs=[
            pl.BlockSpec((tm, d), lambda i, j: (i, 0)),
            pl.BlockSpec((1, d), lambda i, j: (0, 0)),
            pl.BlockSpec((d, fc), lambda i, j: (0, j)),
            pl.BlockSpec((d, fc), lambda i, j: (0, j)),
            pl.BlockSpec((fc, d), lambda i, j: (j, 0)),
            pl.BlockSpec((1, d), lambda i, j: (0, 0)),
        ],
        out_specs=pl.BlockSpec((tm, d), lambda i, j: (i, 0)),
        out_shape=jax.ShapeDtypeStruct((n, d), F32),
        scratch_shapes=[pltpu.VMEM((tm, d), BF16)],
        compiler_params=_params(("arbitrary", "arbitrary")),
        name="ffn_dense",
    )(x, g, w_gate, w_up, w_down, g_final)


def _router_kernel(x_ref, g_ref, wr_ref, h_ref, idx_ref, wt_ref, rank_ref, count_ref, cnt_ref, before_ref):
    h = _rms_rows(x_ref[...], g_ref[...])
    h_hi = h.astype(BF16)
    h_ref[...] = h_hi
    h_lo = (h - h_hi.astype(F32)).astype(BF16)
    wr = wr_ref[...]
    w_hi = wr.astype(BF16)
    w_lo = (wr - w_hi.astype(F32)).astype(BF16)
    nt = (((1,), (1,)), ((), ()))
    logits = (lax.dot_general(w_hi, h_hi, nt, preferred_element_type=F32)
              + lax.dot_general(w_lo, h_hi, nt, preferred_element_type=F32)
              + lax.dot_general(w_hi, h_lo, nt, preferred_element_type=F32))

    ne = logits.shape[0]
    eid = lax.broadcasted_iota(jnp.int32, logits.shape, 0)
    m1 = jnp.max(logits, axis=0, keepdims=True)
    i1 = jnp.min(jnp.where(logits == m1, eid, ne), axis=0, keepdims=True)
    rest = jnp.where(eid == i1, -jnp.inf, logits)
    m2 = jnp.max(rest, axis=0, keepdims=True)
    i2 = jnp.min(jnp.where(rest == m2, eid, ne), axis=0, keepdims=True)
    e = jnp.exp(m2 - m1)
    idx_ref[0:1, :] = i1
    idx_ref[1:2, :] = i2
    wt_ref[0:1, :] = 1.0 / (1.0 + e)
    wt_ref[1:2, :] = e / (1.0 + e)

    tm = logits.shape[1]

    @pl.when(pl.program_id(0) == 0)
    def _():
        cnt_ref[...] = jnp.zeros_like(cnt_ref)
        r = lax.broadcasted_iota(jnp.int32, (tm, tm), 0)
        c = lax.broadcasted_iota(jnp.int32, (tm, tm), 1)
        before_ref[...] = (r < c).astype(BF16)

    oh1 = (eid == i1).astype(F32)
    oh2 = (eid == i2).astype(F32)
    pre1 = jnp.dot(oh1.astype(BF16), before_ref[...], preferred_element_type=F32)
    pre2 = jnp.dot(oh2.astype(BF16), before_ref[...], preferred_element_type=F32)
    tot1 = jnp.sum(oh1, axis=1, keepdims=True)
    tot2 = jnp.sum(oh2, axis=1, keepdims=True)
    base = cnt_ref[:, 0:1]
    rank_ref[0:1, :] = jnp.sum(oh1 * (pre1 + base), axis=0, keepdims=True).astype(jnp.int32)
    rank_ref[1:2, :] = jnp.sum(oh2 * (pre2 + base + tot1), axis=0, keepdims=True).astype(jnp.int32)
    cnt_ref[...] = cnt_ref[...] + (tot1 + tot2)
    count_ref[...] = cnt_ref[...].astype(jnp.int32)


def _router(x, g, w_router_t):
    n, d = x.shape
    tm = RT_TM
    ne = w_router_t.shape[0]
    lanes = 128
    return pl.pallas_call(
        _router_kernel,
        grid=(n // tm,),
        in_specs=[
            pl.BlockSpec((tm, d), lambda i: (i, 0)),
            pl.BlockSpec((1, d), lambda i: (0, 0)),
            pl.BlockSpec(w_router_t.shape, lambda i: (0, 0)),
        ],
        out_specs=[
            pl.BlockSpec((tm, d), lambda i: (i, 0)),
            pl.BlockSpec((TOP_K, tm), lambda i: (0, i)),
            pl.BlockSpec((TOP_K, tm), lambda i: (0, i)),
            pl.BlockSpec((TOP_K, tm), lambda i: (0, i)),
            pl.BlockSpec((ne, lanes), lambda i: (0, 0)),
        ],
        out_shape=[
            jax.ShapeDtypeStruct((n, d), BF16),
            jax.ShapeDtypeStruct((TOP_K, n), jnp.int32),
            jax.ShapeDtypeStruct((TOP_K, n), F32),
            jax.ShapeDtypeStruct((TOP_K, n), jnp.int32),
            jax.ShapeDtypeStruct((ne, lanes), jnp.int32),
        ],
        scratch_shapes=[pltpu.VMEM((ne, lanes), F32), pltpu.VMEM((tm, tm), BF16)],
        compiler_params=_params(("arbitrary",)),
        name="moe_router",
    )(x, g, w_router_t)


ROW_COPY_UNROLL = 8


def _moe_kernel(te_ref, nr_ref, dest_ref, hs_ref, wg_ref, wu_ref, wd_ref, out_ref, ybuf, sem):
    t = pl.program_id(0)
    j = pl.program_id(1)
    nt = pl.num_programs(0)
    nj = pl.num_programs(1)
    tm = hs_ref.shape[0]
    s = t % 2
    valid = nr_ref[t] > 0

    def row_copy(slot, tile, r):
        return pltpu.make_async_copy(ybuf.at[slot, pl.ds(r, 1), :],
                                     out_ref.at[pl.ds(dest_ref[tile * tm + r], 1), :], sem.at[slot])

    def for_rows(n_rows, fn):
        n_groups = n_rows // ROW_COPY_UNROLL

        def group(g, carry):
            for u in range(ROW_COPY_UNROLL):
                fn(g * ROW_COPY_UNROLL + u)
            return carry

        def single(r, carry):
            fn(r)
            return carry

        lax.fori_loop(0, n_groups, group, 0)
        lax.fori_loop(n_groups * ROW_COPY_UNROLL, n_rows, single, 0)

    def wait_tile(tile, slot):
        for_rows(nr_ref[tile], lambda r: row_copy(slot, tile, r).wait())

    @pl.when(j == 0)
    def _():
        @pl.when(t >= 2)
        def _():
            wait_tile(jnp.maximum(t - 2, 0), s)

        @pl.when(valid)
        def _():
            ybuf[s] = jnp.zeros(ybuf.shape[1:], F32)

    @pl.when(valid)
    def _():
        h = hs_ref[...]
        act = _silu(jnp.dot(h, wg_ref[...], preferred_element_type=F32)) * jnp.dot(
            h, wu_ref[...], preferred_element_type=F32)
        ybuf[s] += jnp.dot(act.astype(BF16), wd_ref[...], preferred_element_type=F32)

    @pl.when(j == nj - 1)
    def _():
        for_rows(nr_ref[t], lambda r: row_copy(s, t, r).start())

        @pl.when(t == nt - 1)
        def _():
            wait_tile(nt - 2, 1 - s)
            wait_tile(t, s)


def _moe_gmm(tile_expert, tile_rows, dest, hs, w_gate, w_up, w_down, n_out):
    p, d = hs.shape
    dff = w_gate.shape[2]
    tm, fc = MOE_TM, MOE_FC
    nfc = dff // fc
    assert p // tm >= 2

    def chunk(t, j, nr):
        return jnp.where(nr[t] > 0, j, nfc - 1)

    grid_spec = pltpu.PrefetchScalarGridSpec(
        num_scalar_prefetch=3,
        grid=(p // tm, nfc),
        in_specs=[
            pl.BlockSpec((tm, d), lambda t, j, te, tv, ds: (t, 0)),
            pl.BlockSpec((None, d, fc), lambda t, j, te, tv, ds: (te[t], 0, chunk(t, j, tv))),
            pl.BlockSpec((None, d, fc), lambda t, j, te, tv, ds: (te[t], 0, chunk(t, j, tv))),
            pl.BlockSpec((None, fc, d), lambda t, j, te, tv, ds: (te[t], chunk(t, j, tv), 0)),
        ],
        out_specs=pl.BlockSpec(memory_space=pl.ANY),
        scratch_shapes=[pltpu.VMEM((2, tm, d), F32), pltpu.SemaphoreType.DMA((2,))],
    )
    return pl.pallas_call(
        _moe_kernel,
        grid_spec=grid_spec,
        out_shape=jax.ShapeDtypeStruct((n_out, d), F32),
        compiler_params=_params(("arbitrary", "arbitrary")),
        name="moe_gmm",
    )(tile_expert, tile_rows, dest, hs, w_gate, w_up, w_down)


def _moe_plan(idx, ranks, counts, n_tokens):
    tm = MOE_TM
    n_assign = TOP_K * n_tokens
    n_slots = n_assign + N_EXPERTS * tm
    padded = ((counts + tm - 1) // tm) * tm
    ends = jnp.cumsum(padded)
    offsets = ends - padded
    slot = (jnp.take(offsets, idx) + ranks).reshape(-1)
    assign_of_slot = jnp.full((n_slots,), -1, jnp.int32).at[slot].set(jnp.arange(n_assign, dtype=jnp.int32))
    dest = jnp.maximum(assign_of_slot, 0)
    token_of_slot = dest % n_tokens
    tile_start = jnp.arange(n_slots // tm, dtype=jnp.int32) * tm
    tile_used = tile_start < ends[-1]
    tile_expert = jnp.sum((tile_start[:, None] >= ends[None, :]).astype(jnp.int32), axis=1)
    last_expert = jnp.sum((ends[-1] - 1 >= ends).astype(jnp.int32))
    tile_expert = jnp.where(tile_used, tile_expert, last_expert).astype(jnp.int32)
    group_end = jnp.take(offsets + counts, tile_expert)
    tile_rows = jnp.where(tile_used, jnp.clip(group_end - tile_start, 0, tm), 0).astype(jnp.int32)
    return token_of_slot, dest, tile_expert, tile_rows


def _combine_kernel(x_ref, y1_ref, y2_ref, w_ref, gf_ref, o_ref, *, final_norm):
    w = w_ref[...]
    out = x_ref[...] + w[:, 0:1] * y1_ref[...] + w[:, 1:2] * y2_ref[...]
    if final_norm:
        out = _rms_rows(out, gf_ref[...])
    o_ref[...] = out


def _combine(x, yg, wts_t, g_final, final_norm):
    n, d = x.shape
    tm = CMB_TM
    nb = n // tm
    row = pl.BlockSpec((tm, d), lambda i: (i, 0))
    return pl.pallas_call(
        functools.partial(_combine_kernel, final_norm=final_norm),
        grid=(nb,),
        in_specs=[row, row, pl.BlockSpec((tm, d), lambda i: (i + nb, 0)),
                  pl.BlockSpec((tm, TOP_K), lambda i: (i, 0)), pl.BlockSpec((1, d), lambda i: (0, 0))],
        out_specs=row,
        out_shape=jax.ShapeDtypeStruct((n, d), F32),
        compiler_params=_params(("arbitrary",)),
        name="moe_combine",
    )(x, yg, yg, wts_t, g_final)


def _moe_layer(x, g, w_router, w_gate, w_up, w_down, g_final, final_norm):
    n, d = x.shape
    h, idx, wts, ranks, counts = _router(x, g, w_router.T)
    token_of_slot, dest, tile_expert, tile_rows = _moe_plan(idx, ranks, counts[:, 0], n)
    hs = jnp.take(h, token_of_slot, axis=0)
    yg = _moe_gmm(tile_expert, tile_rows, dest, hs, w_gate, w_up, w_down, TOP_K * n)
    return _combine(x, yg, wts.T, g_final, final_norm)


def kernel(x, norm_mix, w_in, pool_w, pool_scale, w_pool_up, w_attn_up, w_out, norm_ffn, ffn_gate,
           ffn_up, ffn_down, w_router, moe_gate, moe_up, moe_down, norm_final):
    batch, seq, d = x.shape
    depth = norm_mix.shape[0]
    pool_width = pool_scale.shape[1]
    sb_width = w_attn_up.shape[1]
    n = batch * seq
    xf = x.reshape(n, d)
    g_final = norm_final.reshape(1, d)

    for i in range(depth):
        last = i == depth - 1
        u, qkv, gates = _in_proj(xf, norm_mix[i].reshape(1, d), w_in[i].astype(BF16), pool_width, sb_width)
        y_attn = _attention(qkv, batch, seq)
        xf = _mix_out(u, y_attn, gates, xf, pool_w[i].astype(BF16), pool_scale[i].reshape(1, pool_width),
                      w_pool_up[i].astype(BF16), w_attn_up[i].astype(BF16), w_out[i].astype(BF16), seq)
        gf = norm_ffn[i].reshape(1, d)
        j = i // 2
        if i % 2 == 0:
            xf = _ffn(xf, gf, ffn_gate[j].astype(BF16), ffn_up[j].astype(BF16), ffn_down[j].astype(BF16),
                      g_final, last)
        else:
            xf = _moe_layer(xf, gf, w_router[j], moe_gate[j].astype(BF16), moe_up[j].astype(BF16),
                            moe_down[j].astype(BF16), g_final, last)
    return xf.reshape(batch, seq, d)
```

```python
import functools
import math

import jax
import jax.numpy as jnp
from jax import lax
from jax.experimental import pallas as pl
from jax.experimental.pallas import tpu as pltpu

F32 = jnp.float32
BF16 = jnp.bfloat16

RMS_EPS = 1e-6
LOG2E = 1.0 / math.log(2.0)
POOL_WINDOWS = (2, 4, 8, 16)
POOL_HALO = 16
SB_HEADS = 8
SB_HEAD_DIM = 128
N_EXPERTS = 8
TOP_K = 2

VMEM_LIMIT_BYTES = 56 * 1024 * 1024

IN_TM, IN_TN = 1024, 512
ATT_T = 256
ATT_HEADS = 4
MIX_TM = 256
FFN_TM, FFN_FC = 512, 512
RT_TM = 512
MOE_TM, MOE_FC = 512, 512
CMB_TM = 512
NORM_CHUNK = 128


def _params(sem):
    return pltpu.CompilerParams(dimension_semantics=sem, vmem_limit_bytes=VMEM_LIMIT_BYTES)


def _rms_rows(x, g):
    ms = jnp.mean(x * x, axis=-1, keepdims=True)
    return x * lax.rsqrt(ms + RMS_EPS) * g


def _rms_to_scratch(x_ref, g_ref, h_ref, rows):
    g = g_ref[...]

    def body(c, carry):
        r0 = pl.multiple_of(c * NORM_CHUNK, NORM_CHUNK)
        h_ref[pl.ds(r0, NORM_CHUNK), :] = _rms_rows(x_ref[pl.ds(r0, NORM_CHUNK), :], g).astype(h_ref.dtype)
        return carry

    lax.fori_loop(0, rows // NORM_CHUNK, body, 0)


def _in_proj_kernel(x_ref, g_ref, w_ref, u_ref, qkv_ref, gate_ref, h_ref, *, n_pool, n_q, n_qkv, q_scale):
    j = pl.program_id(1)

    @pl.when(j == 0)
    def _():
        _rms_to_scratch(x_ref, g_ref, h_ref, x_ref.shape[0])

    def project():
        return jnp.dot(h_ref[...], w_ref[...], preferred_element_type=F32)

    @pl.when(j < n_pool)
    def _():
        u_ref[...] = project()

    @pl.when((j >= n_pool) & (j < n_pool + n_q))
    def _():
        qkv_ref[...] = (project() * q_scale).astype(BF16)

    @pl.when((j >= n_pool + n_q) & (j < n_pool + n_qkv))
    def _():
        qkv_ref[...] = project().astype(BF16)

    @pl.when(j >= n_pool + n_qkv)
    def _():
        gate_ref[...] = 1.0 / (1.0 + jnp.exp(-project()))


def _in_proj(x, g, w, layer, pool_w, sb_w):
    n, d = x.shape
    cols = w.shape[2]
    gate_w = cols - pool_w - 3 * sb_w
    tm, tn = IN_TM, IN_TN
    n_pool, n_q, n_qkv, n_gate = pool_w // tn, sb_w // tn, 3 * sb_w // tn, gate_w // tn
    kern = functools.partial(_in_proj_kernel, n_pool=n_pool, n_q=n_q, n_qkv=n_qkv,
                             q_scale=1.0 / math.sqrt(SB_HEAD_DIM))
    return pl.pallas_call(
        kern,
        grid=(n // tm, cols // tn),
        in_specs=[
            pl.BlockSpec((tm, d), lambda i, j: (i, 0)),
            pl.BlockSpec((1, d), lambda i, j: (0, 0)),
            pl.BlockSpec((None, d, tn), lambda i, j: (layer, 0, j)),
        ],
        out_specs=[
            pl.BlockSpec((tm, tn), lambda i, j: (i, jnp.minimum(j, n_pool - 1))),
            pl.BlockSpec((tm, tn), lambda i, j: (i, jnp.clip(j - n_pool, 0, n_qkv - 1))),
            pl.BlockSpec((tm, tn), lambda i, j: (i, jnp.maximum(j - n_pool - n_qkv, 0))),
        ],
        out_shape=[
            jax.ShapeDtypeStruct((n, pool_w), F32),
            jax.ShapeDtypeStruct((n, 3 * sb_w), BF16),
            jax.ShapeDtypeStruct((n, gate_w), F32),
        ],
        scratch_shapes=[pltpu.VMEM((tm, d), BF16)],
        compiler_params=_params(("arbitrary", "arbitrary")),
        name="in_proj",
    )(x, g, w)


def _attn_kernel(q_ref, k_ref, v_ref, o_ref, acc_ref):
    t, dh = ATT_T, SB_HEAD_DIM
    qi = pl.program_id(2)

    row = lax.broadcasted_iota(jnp.int32, (t, t), 0)
    col = lax.broadcasted_iota(jnp.int32, (t, t), 1)
    tri = (row >= col).astype(BF16)
    tri2 = jnp.concatenate([tri, tri], axis=0)
    causal = col < row

    heads = range(ATT_HEADS)
    cols = [slice(hh * dh, (hh + 1) * dh) for hh in heads]

    def block(kb, carries, diag):
        k0 = pl.multiple_of(kb * t, t)
        z = [lax.dot_general(q_ref[:, c], k_ref[pl.ds(k0, t), c], (((1,), (1,)), ((), ())),
                             preferred_element_type=F32) for c in cols]
        suf = []
        for hh in heads:
            sp = jnp.maximum(z[hh], 0.0) + jnp.log(1.0 + jnp.exp2(jnp.abs(z[hh]) * -LOG2E))
            if diag:
                sp = jnp.where(causal, sp, 0.0)
            hi = sp.astype(BF16)
            lo = (sp - hi.astype(F32)).astype(BF16)
            suf.append(jnp.dot(jnp.concatenate([hi, lo], axis=1), tri2, preferred_element_type=F32))
        new_carries = []
        for hh in heads:
            if diag:
                a = jnp.where(causal, jnp.exp(z[hh] - suf[hh]), 0.0)
            else:
                a = jnp.exp(z[hh] - suf[hh] - carries[hh])
            pv = jnp.dot(a.astype(BF16), v_ref[pl.ds(k0, t), cols[hh]], preferred_element_type=F32)
            if diag:
                acc_ref[hh] = pv
                new_carries.append(suf[hh][:, 0:1])
            else:
                acc_ref[hh] += pv
                new_carries.append(carries[hh] + suf[hh][:, 0:1])
        return tuple(new_carries)

    carries = block(qi, None, True)
    lax.fori_loop(0, qi, lambda it, c: block(qi - 1 - it, c, False), carries)
    for hh in heads:
        o_ref[:, cols[hh]] = acc_ref[hh].astype(o_ref.dtype)


def _attention(qkv, batch, seq):
    n = qkv.shape[0]
    t, h, dh = ATT_T, SB_HEADS, SB_HEAD_DIM
    nq = seq // t
    hg = h // ATT_HEADS
    w = ATT_HEADS * dh
    return pl.pallas_call(
        _attn_kernel,
        grid=(batch, hg, nq),
        in_specs=[
            pl.BlockSpec((t, w), lambda b, g, i: (b * nq + i, g)),
            pl.BlockSpec((seq, w), lambda b, g, i: (b, hg + g)),
            pl.BlockSpec((seq, w), lambda b, g, i: (b, 2 * hg + g)),
        ],
        out_specs=pl.BlockSpec((t, w), lambda b, g, i: (b * nq + i, g)),
        out_shape=jax.ShapeDtypeStruct((n, h * dh), BF16),
        scratch_shapes=[pltpu.VMEM((ATT_HEADS, t, dh), F32)],
        compiler_params=_params(("arbitrary", "arbitrary", "arbitrary")),
        name="sb_attention",
    )(qkv, qkv, qkv)


def _mix_out_kernel(u_ref, up_ref, ya_ref, gate_ref, x_ref, pw_ref, ps_ref, wpu_ref, wau_ref, wo_ref,
                    o_ref, *, seq):
    tm, pool_w = u_ref.shape
    d = x_ref.shape[1]
    gd = pool_w // len(POOL_WINDOWS)
    i = pl.program_id(0)
    row0 = (i * tm) % seq

    u = u_ref[...]
    halo = jnp.where(row0 > 0, up_ref[...], 0.0)
    pos = (row0 + lax.broadcasted_iota(jnp.int32, (tm, 1), 0) + 1).astype(F32)

    mixed = []
    for gi, w in enumerate(POOL_WINDOWS):
        c0 = gi * gd
        s = jnp.concatenate([halo[:, c0:c0 + gd], u[:, c0:c0 + gd]], axis=0)
        step = 1
        while step < w:
            s = s + pltpu.roll(s, step, axis=0)
            step *= 2
        pooled = s[POOL_HALO:, :] / jnp.minimum(pos, float(w)) - u[:, c0:c0 + gd]
        mixed.append(jnp.dot(pooled.astype(BF16), pw_ref[gi], preferred_element_type=F32))
    y_pool = (jnp.concatenate(mixed, axis=1) * ps_ref[...]).astype(BF16)

    p_up = jnp.dot(y_pool, wpu_ref[...], preferred_element_type=F32)
    a_up = jnp.dot(ya_ref[...], wau_ref[...], preferred_element_type=F32)
    merged = gate_ref[:, :d] * p_up + gate_ref[:, d:] * a_up
    o_ref[...] = x_ref[...] + jnp.dot(merged.astype(BF16), wo_ref[...], preferred_element_type=F32)


def _layer_spec(stacked_shape, layer):
    rest = tuple(stacked_shape[1:])
    return pl.BlockSpec((None,) + rest, lambda i: (layer,) + (0,) * len(rest), pipeline_mode=pl.Buffered(1))


def _mix_out(u, y_attn, gates, x, pool_w, pool_scale, w_pool_up, w_attn_up, w_out, layer, seq):
    n, d = x.shape
    pw = u.shape[1]
    tm = MIX_TM
    hb = tm // POOL_HALO
    return pl.pallas_call(
        functools.partial(_mix_out_kernel, seq=seq),
        grid=(n // tm,),
        in_specs=[
            pl.BlockSpec((tm, pw), lambda i: (i, 0)),
            pl.BlockSpec((POOL_HALO, pw), lambda i: (jnp.maximum(i * hb - 1, 0), 0)),
            pl.BlockSpec((tm, y_attn.shape[1]), lambda i: (i, 0)),
            pl.BlockSpec((tm, 2 * d), lambda i: (i, 0)),
            pl.BlockSpec((tm, d), lambda i: (i, 0)),
            _layer_spec(pool_w.shape, layer),
            _layer_spec(pool_scale.shape, layer),
            _layer_spec(w_pool_up.shape, layer),
            _layer_spec(w_attn_up.shape, layer),
            _layer_spec(w_out.shape, layer),
        ],
        out_specs=pl.BlockSpec((tm, d), lambda i: (i, 0)),
        out_shape=jax.ShapeDtypeStruct((n, d), F32),
        compiler_params=_params(("arbitrary",)),
        name="mix_out",
    )(u, u, y_attn, gates, x, pool_w, pool_scale, w_pool_up, w_attn_up, w_out)


def _silu(x):
    return x / (1.0 + jnp.exp(-x))


def _ffn_kernel(x_ref, g_ref, wg_ref, wu_ref, wd_ref, gf_ref, o_ref, h_ref, *, final_norm):
    j = pl.program_id(1)

    @pl.when(j == 0)
    def _():
        _rms_to_scratch(x_ref, g_ref, h_ref, x_ref.shape[0])
        o_ref[...] = x_ref[...]

    h = h_ref[...]
    act = _silu(jnp.dot(h, wg_ref[...], preferred_element_type=F32)) * jnp.dot(
        h, wu_ref[...], preferred_element_type=F32)
    o_ref[...] += jnp.dot(act.astype(BF16), wd_ref[...], preferred_element_type=F32)

    if final_norm:
        @pl.when(j == pl.num_programs(1) - 1)
        def _():
            o_ref[...] = _rms_rows(o_ref[...], gf_ref[...])


def _ffn(x, g, w_gate, w_up, w_down, layer, g_final, final_norm):
    n, d = x.shape
    dff = w_gate.shape[2]
    tm, fc = FFN_TM, FFN_FC
    return pl.pallas_call(
        functools.partial(_ffn_kernel, final_norm=final_norm),
        grid=(n // tm, dff // fc),
        in_specs=[
            pl.BlockSpec((tm, d), lambda i, j: (i, 0)),
            pl.BlockSpec((1, d), lambda i, j: (0, 0)),
            pl.BlockSpec((None, d, fc), lambda i, j: (layer, 0, j)),
            pl.BlockSpec((None, d, fc), lambda i, j: (layer, 0, j)),
            pl.BlockSpec((None, fc, d), lambda i, j: (layer, j, 0)),
            pl.BlockSpec((1, d), lambda i, j: (0, 0)),
        ],
        out_specs=pl.BlockSpec((tm, d), lambda i, j: (i, 0)),
        out_shape=jax.ShapeDtypeStruct((n, d), F32),
        scratch_shapes=[pltpu.VMEM((tm, d), BF16)],
        compiler_params=_params(("arbitrary", "arbitrary")),
        name="ffn_dense",
    )(x, g, w_gate, w_up, w_down, g_final)


def _router_kernel(x_ref, g_ref, wr_ref, h_ref, idx_ref, wt_ref, rank_ref, count_ref, cnt_ref, before_ref):
    h = _rms_rows(x_ref[...], g_ref[...])
    h_hi = h.astype(BF16)
    half = h.shape[1] // 2
    bits = lax.bitcast_convert_type(h_hi.astype(F32), jnp.uint32)
    h_ref[...] = bits[:, half:] | (bits[:, :half] >> 16)
    h_lo = (h - h_hi.astype(F32)).astype(BF16)
    wr = wr_ref[...]
    w_hi = wr.astype(BF16)
    w_lo = (wr - w_hi.astype(F32)).astype(BF16)
    nt = (((1,), (1,)), ((), ()))
    logits = (lax.dot_general(w_hi, h_hi, nt, preferred_element_type=F32)
              + lax.dot_general(w_lo, h_hi, nt, preferred_element_type=F32)
              + lax.dot_general(w_hi, h_lo, nt, preferred_element_type=F32))

    ne = logits.shape[0]
    eid = lax.broadcasted_iota(jnp.int32, logits.shape, 0)
    m1 = jnp.max(logits, axis=0, keepdims=True)
    i1 = jnp.min(jnp.where(logits == m1, eid, ne), axis=0, keepdims=True)
    rest = jnp.where(eid == i1, -jnp.inf, logits)
    m2 = jnp.max(rest, axis=0, keepdims=True)
    i2 = jnp.min(jnp.where(rest == m2, eid, ne), axis=0, keepdims=True)
    e = jnp.exp(m2 - m1)
    idx_ref[0:1, :] = i1
    idx_ref[1:2, :] = i2
    wt_ref[0:1, :] = 1.0 / (1.0 + e)
    wt_ref[1:2, :] = e / (1.0 + e)

    tm = logits.shape[1]

    @pl.when(pl.program_id(0) == 0)
    def _():
        cnt_ref[...] = jnp.zeros_like(cnt_ref)
        r = lax.broadcasted_iota(jnp.int32, (tm, tm), 0)
        c = lax.broadcasted_iota(jnp.int32, (tm, tm), 1)
        before_ref[...] = (r < c).astype(BF16)

    oh1 = (eid == i1).astype(F32)
    oh2 = (eid == i2).astype(F32)
    pre1 = jnp.dot(oh1.astype(BF16), before_ref[...], preferred_element_type=F32)
    pre2 = jnp.dot(oh2.astype(BF16), before_ref[...], preferred_element_type=F32)
    tot1 = jnp.sum(oh1, axis=1, keepdims=True)
    tot2 = jnp.sum(oh2, axis=1, keepdims=True)
    base = cnt_ref[:, 0:1]
    rank_ref[0:1, :] = jnp.sum(oh1 * (pre1 + base), axis=0, keepdims=True).astype(jnp.int32)
    rank_ref[1:2, :] = jnp.sum(oh2 * (pre2 + base + tot1), axis=0, keepdims=True).astype(jnp.int32)
    cnt_ref[...] = cnt_ref[...] + (tot1 + tot2)
    count_ref[...] = cnt_ref[...].astype(jnp.int32)


def _router(x, g, w_router_t):
    n, d = x.shape
    tm = RT_TM
    ne = w_router_t.shape[0]
    lanes = 128
    return pl.pallas_call(
        _router_kernel,
        grid=(n // tm,),
        in_specs=[
            pl.BlockSpec((tm, d), lambda i: (i, 0)),
            pl.BlockSpec((1, d), lambda i: (0, 0)),
            pl.BlockSpec(w_router_t.shape, lambda i: (0, 0)),
        ],
        out_specs=[
            pl.BlockSpec((tm, d // 2), lambda i: (i, 0)),
            pl.BlockSpec((TOP_K, tm), lambda i: (0, i)),
            pl.BlockSpec((TOP_K, tm), lambda i: (0, i)),
            pl.BlockSpec((TOP_K, tm), lambda i: (0, i)),
            pl.BlockSpec((ne, lanes), lambda i: (0, 0)),
        ],
        out_shape=[
            jax.ShapeDtypeStruct((n, d // 2), jnp.uint32),
            jax.ShapeDtypeStruct((TOP_K, n), jnp.int32),
            jax.ShapeDtypeStruct((TOP_K, n), F32),
            jax.ShapeDtypeStruct((TOP_K, n), jnp.int32),
            jax.ShapeDtypeStruct((ne, lanes), jnp.int32),
        ],
        scratch_shapes=[pltpu.VMEM((ne, lanes), F32), pltpu.VMEM((tm, tm), BF16)],
        compiler_params=_params(("arbitrary",)),
        name="moe_router",
    )(x, g, w_router_t)


ROW_COPY_UNROLL = 8


def _moe_kernel(te_ref, nr_ref, tok_ref, dest_ref, hsrc_ref, wg_ref, wu_ref, wd_ref, out_ref,
                gbuf, hbf, ybuf, gsem, sem):
    t = pl.program_id(0)
    j = pl.program_id(1)
    nt = pl.num_programs(0)
    nj = pl.num_programs(1)
    tm, half = gbuf.shape[1], gbuf.shape[2]
    s = t % 2
    valid = nr_ref[t] > 0

    def fetch_rows(tile, slot):
        def group(g, carry):
            for u in range(ROW_COPY_UNROLL):
                r = g * ROW_COPY_UNROLL + u
                pltpu.make_async_copy(hsrc_ref.at[pl.ds(tok_ref[tile * tm + r], 1), :],
                                      gbuf.at[slot, pl.ds(r, 1), :], gsem.at[slot]).start()
            return carry

        lax.fori_loop(0, tm // ROW_COPY_UNROLL, group, 0)

    def wait_fetch(slot):
        pltpu.make_async_copy(hsrc_ref.at[pl.ds(0, tm), :], gbuf.at[slot], gsem.at[slot]).wait()

    def row_copy(slot, tile, r):
        return pltpu.make_async_copy(ybuf.at[slot, pl.ds(r, 1), :],
                                     out_ref.at[pl.ds(dest_ref[tile * tm + r], 1), :], sem.at[slot])

    def for_rows(n_rows, fn):
        n_groups = n_rows // ROW_COPY_UNROLL

        def group(g, carry):
            for u in range(ROW_COPY_UNROLL):
                fn(g * ROW_COPY_UNROLL + u)
            return carry

        def single(r, carry):
            fn(r)
            return carry

        lax.fori_loop(0, n_groups, group, 0)
        lax.fori_loop(n_groups * ROW_COPY_UNROLL, n_rows, single, 0)

    def wait_tile(tile, slot):
        for_rows(nr_ref[tile], lambda r: row_copy(slot, tile, r).wait())

    @pl.when(j == 0)
    def _():
        @pl.when(t >= 2)
        def _():
            wait_tile(jnp.maximum(t - 2, 0), s)

        @pl.when(valid & (t == 0))
        def _():
            fetch_rows(0, 0)

        @pl.when(valid)
        def _():
            wait_fetch(s)
            words = gbuf[s]
            hbf[:, :half] = lax.bitcast_convert_type(words << 16, F32).astype(BF16)
            hbf[:, half:] = lax.bitcast_convert_type(words & jnp.uint32(0xFFFF0000), F32).astype(BF16)
            ybuf[s] = jnp.zeros(ybuf.shape[1:], F32)

        nxt = jnp.minimum(t + 1, nt - 1)

        @pl.when((t + 1 < nt) & (nr_ref[nxt] > 0))
        def _():
            fetch_rows(nxt, 1 - s)

    @pl.when(valid)
    def _():
        h = hbf[...]
        act = _silu(jnp.dot(h, wg_ref[...], preferred_element_type=F32)) * jnp.dot(
            h, wu_ref[...], preferred_element_type=F32)
        ybuf[s] += jnp.dot(act.astype(BF16), wd_ref[...], preferred_element_type=F32)

    @pl.when(j == nj - 1)
    def _():
        for_rows(nr_ref[t], lambda r: row_copy(s, t, r).start())

        @pl.when(t == nt - 1)
        def _():
            wait_tile(nt - 2, 1 - s)
            wait_tile(t, s)


def _moe_gmm(tile_expert, tile_rows, token_of_slot, dest, h_packed, w_gate, w_up, w_down, layer):
    n, half = h_packed.shape
    d = 2 * half
    p = token_of_slot.shape[0]
    dff = w_gate.shape[3]
    tm, fc = MOE_TM, MOE_FC
    nfc = dff // fc
    assert p // tm >= 2

    def chunk(t, j, nr):
        return jnp.where(nr[t] > 0, j, nfc - 1)

    grid_spec = pltpu.PrefetchScalarGridSpec(
        num_scalar_prefetch=4,
        grid=(p // tm, nfc),
        in_specs=[
            pl.BlockSpec(memory_space=pl.ANY),
            pl.BlockSpec((None, None, d, fc), lambda t, j, te, nr, tk, ds: (layer, te[t], 0, chunk(t, j, nr))),
            pl.BlockSpec((None, None, d, fc), lambda t, j, te, nr, tk, ds: (layer, te[t], 0, chunk(t, j, nr))),
            pl.BlockSpec((None, None, fc, d), lambda t, j, te, nr, tk, ds: (layer, te[t], chunk(t, j, nr), 0)),
        ],
        out_specs=pl.BlockSpec(memory_space=pl.ANY),
        scratch_shapes=[
            pltpu.VMEM((2, tm, half), jnp.uint32),
            pltpu.VMEM((tm, d), BF16),
            pltpu.VMEM((2, tm, d), F32),
            pltpu.SemaphoreType.DMA((2,)),
            pltpu.SemaphoreType.DMA((2,)),
        ],
    )
    return pl.pallas_call(
        _moe_kernel,
        grid_spec=grid_spec,
        out_shape=jax.ShapeDtypeStruct((TOP_K * n, d), F32),
        compiler_params=_params(("arbitrary", "arbitrary")),
        name="moe_gmm",
    )(tile_expert, tile_rows, token_of_slot, dest, h_packed, w_gate, w_up, w_down)


def _moe_plan(idx, ranks, counts, n_tokens):
    tm = MOE_TM
    n_assign = TOP_K * n_tokens
    n_slots = n_assign + N_EXPERTS * tm
    padded = ((counts + tm - 1) // tm) * tm
    ends = jnp.cumsum(padded)
    offsets = ends - padded
    experts = jnp.arange(N_EXPERTS, dtype=jnp.int32)
    slot = (jnp.sum(jnp.where(idx[..., None] == experts, offsets, 0), axis=-1) + ranks).reshape(-1)
    assign_of_slot = jnp.full((n_slots,), -1, jnp.int32).at[slot].set(jnp.arange(n_assign, dtype=jnp.int32))
    dest = jnp.maximum(assign_of_slot, 0)
    token_of_slot = dest % n_tokens
    tile_start = jnp.arange(n_slots // tm, dtype=jnp.int32) * tm
    tile_used = tile_start < ends[-1]
    tile_expert = jnp.sum((tile_start[:, None] >= ends[None, :]).astype(jnp.int32), axis=1)
    last_expert = jnp.sum((ends[-1] - 1 >= ends).astype(jnp.int32))
    tile_expert = jnp.where(tile_used, tile_expert, last_expert).astype(jnp.int32)
    group_end = jnp.sum(jnp.where(tile_expert[:, None] == experts, offsets + counts, 0), axis=1)
    tile_rows = jnp.where(tile_used, jnp.clip(group_end - tile_start, 0, tm), 0).astype(jnp.int32)
    return token_of_slot, dest, tile_expert, tile_rows


def _combine_kernel(x_ref, y1_ref, y2_ref, w_ref, gf_ref, o_ref, *, final_norm):
    w = w_ref[...]
    out = x_ref[...] + w[:, 0:1] * y1_ref[...] + w[:, 1:2] * y2_ref[...]
    if final_norm:
        out = _rms_rows(out, gf_ref[...])
    o_ref[...] = out


def _combine(x, yg, wts_t, g_final, final_norm):
    n, d = x.shape
    tm = CMB_TM
    nb = n // tm
    row = pl.BlockSpec((tm, d), lambda i: (i, 0))
    return pl.pallas_call(
        functools.partial(_combine_kernel, final_norm=final_norm),
        grid=(nb,),
        in_specs=[row, row, pl.BlockSpec((tm, d), lambda i: (i + nb, 0)),
                  pl.BlockSpec((tm, TOP_K), lambda i: (i, 0)), pl.BlockSpec((1, d), lambda i: (0, 0))],
        out_specs=row,
        out_shape=jax.ShapeDtypeStruct((n, d), F32),
        compiler_params=_params(("arbitrary",)),
        name="moe_combine",
    )(x, yg, yg, wts_t, g_final)


def _moe_layer(x, g, w_router, w_gate, w_up, w_down, layer, g_final, final_norm):
    n, d = x.shape
    h_packed, idx, wts, ranks, counts = _router(x, g, w_router.T)
    token_of_slot, dest, tile_expert, tile_rows = _moe_plan(idx, ranks, counts[:, 0], n)
    yg = _moe_gmm(tile_expert, tile_rows, token_of_slot, dest, h_packed, w_gate, w_up, w_down, layer)
    return _combine(x, yg, wts.T, g_final, final_norm)


def kernel(x, norm_mix, w_in, pool_w, pool_scale, w_pool_up, w_attn_up, w_out, norm_ffn, ffn_gate,
           ffn_up, ffn_down, w_router, moe_gate, moe_up, moe_down, norm_final):
    batch, seq, d = x.shape
    depth = norm_mix.shape[0]
    pool_width = pool_scale.shape[1]
    sb_width = w_attn_up.shape[1]
    n = batch * seq
    xf = x.reshape(n, d)
    g_final = norm_final.reshape(1, d)

    (w_in_b, pool_w_b, w_pool_up_b, w_attn_up_b, w_out_b, ffn_gate_b, ffn_up_b, ffn_down_b,
     moe_gate_b, moe_up_b, moe_down_b) = (
        a.astype(BF16) for a in (w_in, pool_w, w_pool_up, w_attn_up, w_out, ffn_gate, ffn_up, ffn_down,
                                 moe_gate, moe_up, moe_down))
    pool_scale_r = pool_scale.reshape(depth, 1, pool_width)

    for i in range(depth):
        last = i == depth - 1
        u, qkv, gates = _in_proj(xf, norm_mix[i].reshape(1, d), w_in_b, i, pool_width, sb_width)
        y_attn = _attention(qkv, batch, seq)
        xf = _mix_out(u, y_attn, gates, xf, pool_w_b, pool_scale_r, w_pool_up_b, w_attn_up_b, w_out_b, i, seq)
        gf = norm_ffn[i].reshape(1, d)
        j = i // 2
        if i % 2 == 0:
            xf = _ffn(xf, gf, ffn_gate_b, ffn_up_b, ffn_down_b, j, g_final, last)
        else:
            xf = _moe_layer(xf, gf, w_router[j], moe_gate_b, moe_up_b, moe_down_b, j, g_final, last)
    return xf.reshape(batch, seq, d)
```

```python
import functools
import math

import jax
import jax.numpy as jnp
from jax import lax
from jax.experimental import pallas as pl
from jax.experimental.pallas import tpu as pltpu

F32 = jnp.float32
BF16 = jnp.bfloat16

RMS_EPS = 1e-6
LOG2E = 1.0 / math.log(2.0)
POOL_WINDOWS = (2, 4, 8, 16)
POOL_HALO = 16
SB_HEADS = 8
SB_HEAD_DIM = 128
N_EXPERTS = 8
TOP_K = 2

VMEM_LIMIT_BYTES = 56 * 1024 * 1024

IN_TM, IN_TN = 1024, 512
ATT_T = 256
ATT_HEADS = 4
MIX_TM = 256
FFN_TM, FFN_FC = 512, 512
RT_TM = 512
MOE_TM, MOE_FC = 512, 512
CMB_TM = 512
NORM_CHUNK = 128


def _params(sem):
    return pltpu.CompilerParams(dimension_semantics=sem, vmem_limit_bytes=VMEM_LIMIT_BYTES)


def _rms_rows(x, g):
    ms = jnp.mean(x * x, axis=-1, keepdims=True)
    return x * lax.rsqrt(ms + RMS_EPS) * g


def _rms_to_scratch(x_ref, g_ref, h_ref, rows):
    g = g_ref[...]

    def body(c, carry):
        r0 = pl.multiple_of(c * NORM_CHUNK, NORM_CHUNK)
        h_ref[pl.ds(r0, NORM_CHUNK), :] = _rms_rows(x_ref[pl.ds(r0, NORM_CHUNK), :], g).astype(h_ref.dtype)
        return carry

    lax.fori_loop(0, rows // NORM_CHUNK, body, 0)


def _in_proj_kernel(x_ref, g_ref, w_ref, u_ref, qkv_ref, gate_ref, h_ref, *, n_pool, n_q, n_qkv, q_scale):
    j = pl.program_id(1)

    @pl.when(j == 0)
    def _():
        _rms_to_scratch(x_ref, g_ref, h_ref, x_ref.shape[0])

    def project():
        return jnp.dot(h_ref[...], w_ref[...], preferred_element_type=F32)

    @pl.when(j < n_pool)
    def _():
        u_ref[...] = project()

    @pl.when((j >= n_pool) & (j < n_pool + n_q))
    def _():
        qkv_ref[...] = (project() * q_scale).astype(BF16)

    @pl.when((j >= n_pool + n_q) & (j < n_pool + n_qkv))
    def _():
        qkv_ref[...] = project().astype(BF16)

    @pl.when(j >= n_pool + n_qkv)
    def _():
        gate_ref[...] = 1.0 / (1.0 + jnp.exp(-project()))


def _in_proj(x, g, w, layer, pool_w, sb_w):
    n, d = x.shape
    cols = w.shape[2]
    gate_w = cols - pool_w - 3 * sb_w
    tm, tn = IN_TM, IN_TN
    n_pool, n_q, n_qkv, n_gate = pool_w // tn, sb_w // tn, 3 * sb_w // tn, gate_w // tn
    kern = functools.partial(_in_proj_kernel, n_pool=n_pool, n_q=n_q, n_qkv=n_qkv,
                             q_scale=1.0 / math.sqrt(SB_HEAD_DIM))
    return pl.pallas_call(
        kern,
        grid=(n // tm, cols // tn),
        in_specs=[
            pl.BlockSpec((tm, d), lambda i, j: (i, 0)),
            pl.BlockSpec((1, d), lambda i, j: (0, 0)),
            pl.BlockSpec((None, d, tn), lambda i, j: (layer, 0, j)),
        ],
        out_specs=[
            pl.BlockSpec((tm, tn), lambda i, j: (i, jnp.minimum(j, n_pool - 1))),
            pl.BlockSpec((tm, tn), lambda i, j: (i, jnp.clip(j - n_pool, 0, n_qkv - 1))),
            pl.BlockSpec((tm, tn), lambda i, j: (i, jnp.maximum(j - n_pool - n_qkv, 0))),
        ],
        out_shape=[
            jax.ShapeDtypeStruct((n, pool_w), F32),
            jax.ShapeDtypeStruct((n, 3 * sb_w), BF16),
            jax.ShapeDtypeStruct((n, gate_w), F32),
        ],
        scratch_shapes=[pltpu.VMEM((tm, d), BF16)],
        compiler_params=_params(("arbitrary", "arbitrary")),
        name="in_proj",
    )(x, g, w)


def _attn_kernel(*refs, n_cast):
    q_ref, k_ref, v_ref = refs[:3]
    cast_in = refs[3:3 + n_cast]
    o_ref = refs[3 + n_cast]
    cast_out = refs[4 + n_cast:4 + 2 * n_cast]
    acc_ref, z0_ref, z1_ref, carry_ref = refs[4 + 2 * n_cast:]
    for src, dst in zip(cast_in, cast_out):
        dst[...] = src[...].astype(dst.dtype)

    t, dh = ATT_T, SB_HEAD_DIM
    qi = pl.program_id(2)

    row = lax.broadcasted_iota(jnp.int32, (t, t), 0)
    col = lax.broadcasted_iota(jnp.int32, (t, t), 1)
    tri = (row >= col).astype(BF16)
    causal = col < row

    heads = range(ATT_HEADS)
    cols = [slice(hh * dh, (hh + 1) * dh) for hh in heads]

    z_refs = (z0_ref, z1_ref)

    def scores_into(kb, slot):
        k0 = pl.multiple_of(kb * t, t)
        for hh in heads:
            z_refs[slot][hh] = lax.dot_general(q_ref[:, cols[hh]], k_ref[pl.ds(k0, t), cols[hh]],
                                               (((1,), (1,)), ((), ())), preferred_element_type=F32)

    def block(kb, slot, diag=False, ahead=True):
        if ahead:
            scores_into(jnp.maximum(kb - 1, 0), 1 - slot)
        k0 = pl.multiple_of(kb * t, t)
        suf = []
        for hh in heads:
            z = z_refs[slot][hh]
            sp = jnp.maximum(z, 0.0) + jnp.log(1.0 + jnp.exp2(jnp.abs(z) * -LOG2E))
            if diag:
                sp = jnp.where(causal, sp, 0.0)
            suf.append(jnp.dot(sp.astype(BF16), tri, preferred_element_type=F32))
        for hh in heads:
            z = z_refs[slot][hh]
            if diag:
                a = jnp.where(causal, jnp.exp(z - suf[hh]), 0.0)
            else:
                a = jnp.exp(z - suf[hh] - carry_ref[hh])
            pv = jnp.dot(a.astype(BF16), v_ref[pl.ds(k0, t), cols[hh]], preferred_element_type=F32)
            if diag:
                acc_ref[hh] = pv
                carry_ref[hh] = suf[hh][:, 0:1]
            else:
                acc_ref[hh] += pv
                carry_ref[hh] += suf[hh][:, 0:1]

    scores_into(qi, 0)
    block(qi, 0, diag=True)

    def pair(p, carry):
        kb = qi - 1 - 2 * p
        block(kb, 1)
        block(kb - 1, 0)
        return carry

    lax.fori_loop(0, qi // 2, pair, 0)

    @pl.when(qi % 2 == 1)
    def _():
        block(0, 1, ahead=False)

    for hh in heads:
        o_ref[:, cols[hh]] = acc_ref[hh].astype(o_ref.dtype)


def _attention(qkv, batch, seq, cast=()):
    n = qkv.shape[0]
    t, h, dh = ATT_T, SB_HEADS, SB_HEAD_DIM
    nq = seq // t
    hg = h // ATT_HEADS
    w = ATT_HEADS * dh
    steps = batch * hg * nq
    cast_in, cast_out, cast_shapes = [], [], []
    for a, first_row, n_rows in cast:
        rows = n_rows // steps
        assert rows * steps == n_rows and rows % 16 == 0 and first_row % rows == 0, (a.shape, first_row, n_rows)
        first = first_row // rows
        cast_in.append(pl.BlockSpec((rows, a.shape[1]),
                                    lambda b, g, i, first=first: (first + (b * hg + g) * nq + i, 0)))
        cast_out.append(pl.BlockSpec((rows, a.shape[1]), lambda b, g, i: ((b * hg + g) * nq + i, 0)))
        cast_shapes.append(jax.ShapeDtypeStruct((n_rows, a.shape[1]), BF16))
    outs = pl.pallas_call(
        functools.partial(_attn_kernel, n_cast=len(cast)),
        grid=(batch, hg, nq),
        in_specs=[
            pl.BlockSpec((t, w), lambda b, g, i: (b * nq + i, g)),
            pl.BlockSpec((seq, w), lambda b, g, i: (b, hg + g)),
            pl.BlockSpec((seq, w), lambda b, g, i: (b, 2 * hg + g)),
        ] + cast_in,
        out_specs=[pl.BlockSpec((t, w), lambda b, g, i: (b * nq + i, g))] + cast_out,
        out_shape=[jax.ShapeDtypeStruct((n, h * dh), BF16)] + cast_shapes,
        scratch_shapes=[
            pltpu.VMEM((ATT_HEADS, t, dh), F32),
            pltpu.VMEM((ATT_HEADS, t, t), F32),
            pltpu.VMEM((ATT_HEADS, t, t), F32),
            pltpu.VMEM((ATT_HEADS, t, 1), F32),
        ],
        compiler_params=_params(("arbitrary", "arbitrary", "arbitrary")),
        name="sb_attention",
    )(qkv, qkv, qkv, *(a for a, _, _ in cast))
    return outs[0], tuple(outs[1:])


def _mix_out_kernel(u_ref, up_ref, ya_ref, gate_ref, x_ref, pw_ref, ps_ref, wpu_ref, wau_ref, wo_ref,
                    o_ref, *, seq):
    tm, pool_w = u_ref.shape
    d = x_ref.shape[1]
    gd = pool_w // len(POOL_WINDOWS)
    i = pl.program_id(0)
    row0 = (i * tm) % seq

    u = u_ref[...]
    halo = jnp.where(row0 > 0, up_ref[...], 0.0)
    pos = (row0 + lax.broadcasted_iota(jnp.int32, (tm, 1), 0) + 1).astype(F32)

    mixed = []
    for gi, w in enumerate(POOL_WINDOWS):
        c0 = gi * gd
        s = jnp.concatenate([halo[:, c0:c0 + gd], u[:, c0:c0 + gd]], axis=0)
        step = 1
        while step < w:
            s = s + pltpu.roll(s, step, axis=0)
            step *= 2
        pooled = s[POOL_HALO:, :] / jnp.minimum(pos, float(w)) - u[:, c0:c0 + gd]
        mixed.append(jnp.dot(pooled.astype(BF16), pw_ref[gi], preferred_element_type=F32))
    y_pool = (jnp.concatenate(mixed, axis=1) * ps_ref[...]).astype(BF16)

    p_up = jnp.dot(y_pool, wpu_ref[...], preferred_element_type=F32)
    a_up = jnp.dot(ya_ref[...], wau_ref[...], preferred_element_type=F32)
    merged = gate_ref[:, :d] * p_up + gate_ref[:, d:] * a_up
    o_ref[...] = x_ref[...] + jnp.dot(merged.astype(BF16), wo_ref[...], preferred_element_type=F32)


def _layer_spec(stacked_shape, layer):
    rest = tuple(stacked_shape[1:])
    return pl.BlockSpec((None,) + rest, lambda i: (layer,) + (0,) * len(rest), pipeline_mode=pl.Buffered(1))


def _mix_out(u, y_attn, gates, x, pool_w, pool_scale, w_pool_up, w_attn_up, w_out, layer, seq):
    n, d = x.shape
    pw = u.shape[1]
    tm = MIX_TM
    hb = tm // POOL_HALO
    return pl.pallas_call(
        functools.partial(_mix_out_kernel, seq=seq),
        grid=(n // tm,),
        in_specs=[
            pl.BlockSpec((tm, pw), lambda i: (i, 0)),
            pl.BlockSpec((POOL_HALO, pw), lambda i: (jnp.maximum(i * hb - 1, 0), 0)),
            pl.BlockSpec((tm, y_attn.shape[1]), lambda i: (i, 0)),
            pl.BlockSpec((tm, 2 * d), lambda i: (i, 0)),
            pl.BlockSpec((tm, d), lambda i: (i, 0)),
            _layer_spec(pool_w.shape, layer),
            _layer_spec(pool_scale.shape, layer),
            _layer_spec(w_pool_up.shape, layer),
            _layer_spec(w_attn_up.shape, layer),
            _layer_spec(w_out.shape, layer),
        ],
        out_specs=pl.BlockSpec((tm, d), lambda i: (i, 0)),
        out_shape=jax.ShapeDtypeStruct((n, d), F32),
        compiler_params=_params(("arbitrary",)),
        name="mix_out",
    )(u, u, y_attn, gates, x, pool_w, pool_scale, w_pool_up, w_attn_up, w_out)


def _silu(x):
    return x / (1.0 + jnp.exp(-x))


def _ffn_kernel(x_ref, g_ref, wg_ref, wu_ref, wd_ref, gf_ref, o_ref, h_ref, *, final_norm):
    j = pl.program_id(1)

    @pl.when(j == 0)
    def _():
        _rms_to_scratch(x_ref, g_ref, h_ref, x_ref.shape[0])
        o_ref[...] = x_ref[...]

    h = h_ref[...]
    act = _silu(jnp.dot(h, wg_ref[...], preferred_element_type=F32)) * jnp.dot(
        h, wu_ref[...], preferred_element_type=F32)
    o_ref[...] += jnp.dot(act.astype(BF16), wd_ref[...], preferred_element_type=F32)

    if final_norm:
        @pl.when(j == pl.num_programs(1) - 1)
        def _():
            o_ref[...] = _rms_rows(o_ref[...], gf_ref[...])


def _ffn(x, g, w_gate, w_up, w_down, layer, g_final, final_norm):
    n, d = x.shape
    dff = w_gate.shape[2]
    tm, fc = FFN_TM, FFN_FC
    return pl.pallas_call(
        functools.partial(_ffn_kernel, final_norm=final_norm),
        grid=(n // tm, dff // fc),
        in_specs=[
            pl.BlockSpec((tm, d), lambda i, j: (i, 0)),
            pl.BlockSpec((1, d), lambda i, j: (0, 0)),
            pl.BlockSpec((None, d, fc), lambda i, j: (layer, 0, j)),
            pl.BlockSpec((None, d, fc), lambda i, j: (layer, 0, j)),
            pl.BlockSpec((None, fc, d), lambda i, j: (layer, j, 0)),
            pl.BlockSpec((1, d), lambda i, j: (0, 0)),
        ],
        out_specs=pl.BlockSpec((tm, d), lambda i, j: (i, 0)),
        out_shape=jax.ShapeDtypeStruct((n, d), F32),
        scratch_shapes=[pltpu.VMEM((tm, d), BF16)],
        compiler_params=_params(("arbitrary", "arbitrary")),
        name="ffn_dense",
    )(x, g, w_gate, w_up, w_down, g_final)


def _router_kernel(x_ref, g_ref, wr_ref, h_ref, idx_ref, wt_ref, rank_ref, count_ref, cnt_ref, before_ref):
    h = _rms_rows(x_ref[...], g_ref[...])
    h_ref[...] = h
    h_hi = h.astype(BF16)
    h_lo = (h - h_hi.astype(F32)).astype(BF16)
    wr = wr_ref[...]
    w_hi = wr.astype(BF16)
    w_lo = (wr - w_hi.astype(F32)).astype(BF16)
    nt = (((1,), (1,)), ((), ()))
    logits = (lax.dot_general(w_hi, h_hi, nt, preferred_element_type=F32)
              + lax.dot_general(w_lo, h_hi, nt, preferred_element_type=F32)
              + lax.dot_general(w_hi, h_lo, nt, preferred_element_type=F32))

    ne = logits.shape[0]
    eid = lax.broadcasted_iota(jnp.int32, logits.shape, 0)
    m1 = jnp.max(logits, axis=0, keepdims=True)
    i1 = jnp.min(jnp.where(logits == m1, eid, ne), axis=0, keepdims=True)
    rest = jnp.where(eid == i1, -jnp.inf, logits)
    m2 = jnp.max(rest, axis=0, keepdims=True)
    i2 = jnp.min(jnp.where(rest == m2, eid, ne), axis=0, keepdims=True)
    e = jnp.exp(m2 - m1)
    idx_ref[0:1, :] = i1
    idx_ref[1:2, :] = i2
    wt_ref[0:1, :] = 1.0 / (1.0 + e)
    wt_ref[1:2, :] = e / (1.0 + e)

    tm = logits.shape[1]

    @pl.when(pl.program_id(0) == 0)
    def _():
        cnt_ref[...] = jnp.zeros_like(cnt_ref)
        r = lax.broadcasted_iota(jnp.int32, (tm, tm), 0)
        c = lax.broadcasted_iota(jnp.int32, (tm, tm), 1)
        before_ref[...] = (r < c).astype(BF16)

    oh1 = (eid == i1).astype(F32)
    oh2 = (eid == i2).astype(F32)
    pre1 = jnp.dot(oh1.astype(BF16), before_ref[...], preferred_element_type=F32)
    pre2 = jnp.dot(oh2.astype(BF16), before_ref[...], preferred_element_type=F32)
    tot1 = jnp.sum(oh1, axis=1, keepdims=True)
    tot2 = jnp.sum(oh2, axis=1, keepdims=True)
    base = cnt_ref[:, 0:1]
    rank_ref[0:1, :] = jnp.sum(oh1 * (pre1 + base), axis=0, keepdims=True).astype(jnp.int32)
    rank_ref[1:2, :] = jnp.sum(oh2 * (pre2 + base + tot1), axis=0, keepdims=True).astype(jnp.int32)
    cnt_ref[...] = cnt_ref[...] + (tot1 + tot2)
    count_ref[...] = cnt_ref[...].astype(jnp.int32)


def _router(x, g, w_router_t):
    n, d = x.shape
    tm = RT_TM
    ne = w_router_t.shape[0]
    lanes = 128
    return pl.pallas_call(
        _router_kernel,
        grid=(n // tm,),
        in_specs=[
            pl.BlockSpec((tm, d), lambda i: (i, 0)),
            pl.BlockSpec((1, d), lambda i: (0, 0)),
            pl.BlockSpec(w_router_t.shape, lambda i: (0, 0)),
        ],
        out_specs=[
            pl.BlockSpec((tm, d), lambda i: (i, 0)),
            pl.BlockSpec((TOP_K, tm), lambda i: (0, i)),
            pl.BlockSpec((TOP_K, tm), lambda i: (0, i)),
            pl.BlockSpec((TOP_K, tm), lambda i: (0, i)),
            pl.BlockSpec((ne, lanes), lambda i: (0, 0)),
        ],
        out_shape=[
            jax.ShapeDtypeStruct((n, d), F32),
            jax.ShapeDtypeStruct((TOP_K, n), jnp.int32),
            jax.ShapeDtypeStruct((TOP_K, n), F32),
            jax.ShapeDtypeStruct((TOP_K, n), jnp.int32),
            jax.ShapeDtypeStruct((ne, lanes), jnp.int32),
        ],
        scratch_shapes=[pltpu.VMEM((ne, lanes), F32), pltpu.VMEM((tm, tm), BF16)],
        compiler_params=_params(("arbitrary",)),
        name="moe_router",
    )(x, g, w_router_t)


ROW_COPY_UNROLL = 8


def _moe_kernel(te_ref, nr_ref, tok_ref, dest_ref, hsrc_ref, wg_ref, wul_ref, wuh_ref, wd_ref, out_ref,
                gbuf, hbf, ybuf, gsem, sem, *, ne_lo):
    t = pl.program_id(0)
    j = pl.program_id(1)
    nt = pl.num_programs(0)
    nj = pl.num_programs(1)
    tm = gbuf.shape[1]
    s = t % 2
    valid = nr_ref[t] > 0

    def fetch_rows(tile, slot):
        def group(g, carry):
            for u in range(ROW_COPY_UNROLL):
                r = g * ROW_COPY_UNROLL + u
                pltpu.make_async_copy(hsrc_ref.at[pl.ds(tok_ref[tile * tm + r], 1), :],
                                      gbuf.at[slot, pl.ds(r, 1), :], gsem.at[slot]).start()
            return carry

        lax.fori_loop(0, tm // ROW_COPY_UNROLL, group, 0)

    def wait_fetch(slot):
        pltpu.make_async_copy(hsrc_ref.at[pl.ds(0, tm), :], gbuf.at[slot], gsem.at[slot]).wait()

    def row_copy(slot, tile, r):
        return pltpu.make_async_copy(ybuf.at[slot, pl.ds(r, 1), :],
                                     out_ref.at[pl.ds(dest_ref[tile * tm + r], 1), :], sem.at[slot])

    def for_rows(n_rows, fn):
        n_groups = n_rows // ROW_COPY_UNROLL

        def group(g, carry):
            for u in range(ROW_COPY_UNROLL):
                fn(g * ROW_COPY_UNROLL + u)
            return carry

        def single(r, carry):
            fn(r)
            return carry

        lax.fori_loop(0, n_groups, group, 0)
        lax.fori_loop(n_groups * ROW_COPY_UNROLL, n_rows, single, 0)

    def wait_tile(tile, slot):
        for_rows(nr_ref[tile], lambda r: row_copy(slot, tile, r).wait())

    @pl.when(j == 0)
    def _():
        @pl.when(t >= 2)
        def _():
            wait_tile(jnp.maximum(t - 2, 0), s)

        @pl.when(valid & (t == 0))
        def _():
            fetch_rows(0, 0)

        @pl.when(valid)
        def _():
            wait_fetch(s)
            hbf[...] = gbuf[s].astype(BF16)
            ybuf[s] = jnp.zeros(ybuf.shape[1:], F32)

        nxt = jnp.minimum(t + 1, nt - 1)

        @pl.when((t + 1 < nt) & (nr_ref[nxt] > 0))
        def _():
            fetch_rows(nxt, 1 - s)

    @pl.when(valid)
    def _():
        h = hbf[...]
        w_up = jnp.where(te_ref[t] < ne_lo, wul_ref[...], wuh_ref[...])
        act = _silu(jnp.dot(h, wg_ref[...], preferred_element_type=F32)) * jnp.dot(
            h, w_up, preferred_element_type=F32)
        ybuf[s] += jnp.dot(act.astype(BF16), wd_ref[...], preferred_element_type=F32)

    @pl.when(j == nj - 1)
    def _():
        for_rows(nr_ref[t], lambda r: row_copy(s, t, r).start())

        @pl.when(t == nt - 1)
        def _():
            wait_tile(nt - 2, 1 - s)
            wait_tile(t, s)


def _moe_gmm(tile_expert, tile_rows, token_of_slot, dest, h, w_gate, w_up_lo, w_up_hi, w_down):
    n, d = h.shape
    p = token_of_slot.shape[0]
    dff = w_gate.shape[2]
    ne_lo = w_up_lo.shape[0]
    tm, fc = MOE_TM, MOE_FC
    nfc = dff // fc
    assert p // tm >= 2

    def chunk(t, j, nr):
        return jnp.where(nr[t] > 0, j, nfc - 1)

    def up_lo_block(t, j, te, nr, tk, ds):
        lo = te[t] < ne_lo
        return (jnp.minimum(te[t], ne_lo - 1), 0, jnp.where(lo, chunk(t, j, nr), nfc - 1))

    def up_hi_block(t, j, te, nr, tk, ds):
        hi = te[t] >= ne_lo
        return (jnp.maximum(te[t] - ne_lo, 0), 0, jnp.where(hi, chunk(t, j, nr), 0))

    grid_spec = pltpu.PrefetchScalarGridSpec(
        num_scalar_prefetch=4,
        grid=(p // tm, nfc),
        in_specs=[
            pl.BlockSpec(memory_space=pl.ANY),
            pl.BlockSpec((None, d, fc), lambda t, j, te, nr, tk, ds: (te[t], 0, chunk(t, j, nr))),
            pl.BlockSpec((None, d, fc), up_lo_block),
            pl.BlockSpec((None, d, fc), up_hi_block),
            pl.BlockSpec((None, fc, d), lambda t, j, te, nr, tk, ds: (te[t], chunk(t, j, nr), 0)),
        ],
        out_specs=pl.BlockSpec(memory_space=pl.ANY),
        scratch_shapes=[
            pltpu.VMEM((2, tm, d), F32),
            pltpu.VMEM((tm, d), BF16),
            pltpu.VMEM((2, tm, d), F32),
            pltpu.SemaphoreType.DMA((2,)),
            pltpu.SemaphoreType.DMA((2,)),
        ],
    )
    return pl.pallas_call(
        functools.partial(_moe_kernel, ne_lo=ne_lo),
        grid_spec=grid_spec,
        out_shape=jax.ShapeDtypeStruct((TOP_K * n, d), F32),
        compiler_params=_params(("arbitrary", "arbitrary")),
        name="moe_gmm",
    )(tile_expert, tile_rows, token_of_slot, dest, h, w_gate, w_up_lo, w_up_hi, w_down)


def _moe_plan(idx, ranks, counts, n_tokens):
    tm = MOE_TM
    n_assign = TOP_K * n_tokens
    n_slots = n_assign + N_EXPERTS * tm
    padded = ((counts + tm - 1) // tm) * tm
    ends = jnp.cumsum(padded)
    offsets = ends - padded
    experts = jnp.arange(N_EXPERTS, dtype=jnp.int32)
    slot = (jnp.sum(jnp.where(idx[..., None] == experts, offsets, 0), axis=-1) + ranks).reshape(-1)
    assign_of_slot = jnp.full((n_slots,), -1, jnp.int32).at[slot].set(jnp.arange(n_assign, dtype=jnp.int32))
    dest = jnp.maximum(assign_of_slot, 0)
    token_of_slot = dest % n_tokens
    tile_start = jnp.arange(n_slots // tm, dtype=jnp.int32) * tm
    tile_used = tile_start < ends[-1]
    tile_expert = jnp.sum((tile_start[:, None] >= ends[None, :]).astype(jnp.int32), axis=1)
    last_expert = jnp.sum((ends[-1] - 1 >= ends).astype(jnp.int32))
    tile_expert = jnp.where(tile_used, tile_expert, last_expert).astype(jnp.int32)
    group_end = jnp.sum(jnp.where(tile_expert[:, None] == experts, offsets + counts, 0), axis=1)
    tile_rows = jnp.where(tile_used, jnp.clip(group_end - tile_start, 0, tm), 0).astype(jnp.int32)
    return token_of_slot, dest, tile_expert, tile_rows


def _combine_kernel(x_ref, y1_ref, y2_ref, w_ref, gf_ref, o_ref, *, final_norm):
    w = w_ref[...]
    out = x_ref[...] + w[:, 0:1] * y1_ref[...] + w[:, 1:2] * y2_ref[...]
    if final_norm:
        out = _rms_rows(out, gf_ref[...])
    o_ref[...] = out


def _combine(x, yg, wts_t, g_final, final_norm):
    n, d = x.shape
    tm = CMB_TM
    nb = n // tm
    row = pl.BlockSpec((tm, d), lambda i: (i, 0))
    return pl.pallas_call(
        functools.partial(_combine_kernel, final_norm=final_norm),
        grid=(nb,),
        in_specs=[row, row, pl.BlockSpec((tm, d), lambda i: (i + nb, 0)),
                  pl.BlockSpec((tm, TOP_K), lambda i: (i, 0)), pl.BlockSpec((1, d), lambda i: (0, 0))],
        out_specs=row,
        out_shape=jax.ShapeDtypeStruct((n, d), F32),
        compiler_params=_params(("arbitrary",)),
        name="moe_combine",
    )(x, yg, yg, wts_t, g_final)


def _moe_layer(x, g, w_router, w_gate, w_up_lo, w_up_hi, w_down, g_final, final_norm):
    n, d = x.shape
    h, idx, wts, ranks, counts = _router(x, g, w_router.T)
    token_of_slot, dest, tile_expert, tile_rows = _moe_plan(idx, ranks, counts[:, 0], n)
    yg = _moe_gmm(tile_expert, tile_rows, token_of_slot, dest, h, w_gate, w_up_lo, w_up_hi, w_down)
    return _combine(x, yg, wts.T, g_final, final_norm)


def kernel(x, norm_mix, w_in, pool_w, pool_scale, w_pool_up, w_attn_up, w_out, norm_ffn, ffn_gate,
           ffn_up, ffn_down, w_router, moe_gate, moe_up, moe_down, norm_final):
    batch, seq, d = x.shape
    depth = norm_mix.shape[0]
    pool_width = pool_scale.shape[1]
    sb_width = w_attn_up.shape[1]
    n = batch * seq
    xf = x.reshape(n, d)
    g_final = norm_final.reshape(1, d)

    w_in_b, pool_w_b, w_pool_up_b, w_attn_up_b, w_out_b, ffn_gate_b, ffn_up_b, ffn_down_b = (
        a.astype(BF16) for a in (w_in, pool_w, w_pool_up, w_attn_up, w_out, ffn_gate, ffn_up, ffn_down))
    pool_scale_r = pool_scale.reshape(depth, 1, pool_width)
    n_exp, _, dff = moe_gate.shape[1:]
    ne_lo = n_exp // 2
    moe_gate2, moe_up2, moe_down2 = (a.reshape(-1, a.shape[-1]) for a in (moe_gate, moe_up, moe_down))

    moe_w = None
    for i in range(depth):
        last = i == depth - 1
        j = i // 2
        u, qkv, gates = _in_proj(xf, norm_mix[i].reshape(1, d), w_in_b, i, pool_width, sb_width)
        if 2 * j + 1 >= depth:
            cast = ()
        elif i % 2 == 0:
            cast = ((moe_gate2, j * n_exp * d, n_exp * d), (moe_up2, j * n_exp * d, ne_lo * d))
        else:
            cast = ((moe_down2, j * n_exp * dff, n_exp * dff),
                    (moe_up2, (j * n_exp + ne_lo) * d, (n_exp - ne_lo) * d))
        y_attn, cast_b = _attention(qkv, batch, seq, cast)
        xf = _mix_out(u, y_attn, gates, xf, pool_w_b, pool_scale_r, w_pool_up_b, w_attn_up_b, w_out_b, i, seq)
        gf = norm_ffn[i].reshape(1, d)
        if i % 2 == 0:
            if cast_b:
                moe_w = (cast_b[0].reshape(n_exp, d, dff), cast_b[1].reshape(ne_lo, d, dff))
            xf = _ffn(xf, gf, ffn_gate_b, ffn_up_b, ffn_down_b, j, g_final, last)
        else:
            w_gate_b, w_up_lo_b = moe_w
            w_down_b, w_up_hi_b = cast_b[0].reshape(n_exp, dff, d), cast_b[1].reshape(n_exp - ne_lo, d, dff)
            xf = _moe_layer(xf, gf, w_router[j], w_gate_b, w_up_lo_b, w_up_hi_b, w_down_b, g_final, last)
    return xf.reshape(batch, seq, d)
```

```python
import functools
import math

import jax
import jax.numpy as jnp
from jax import lax
from jax.experimental import pallas as pl
from jax.experimental.pallas import tpu as pltpu

F32 = jnp.float32
BF16 = jnp.bfloat16

RMS_EPS = 1e-6
LOG2E = 1.0 / math.log(2.0)
POOL_WINDOWS = (2, 4, 8, 16)
POOL_HALO = 16
SB_HEADS = 8
SB_HEAD_DIM = 128
N_EXPERTS = 8
TOP_K = 2

VMEM_LIMIT_BYTES = 56 * 1024 * 1024

IN_TM, IN_TN = 1024, 512
ATT_T = 256
ATT_HEADS = 4
MIX_TM = 256
FFN_TM, FFN_FC = 512, 512
RT_TM = 512
MOE_TM, MOE_FC = 528, 512
CMB_TM = 512
NORM_CHUNK = 128


def _params(sem):
    return pltpu.CompilerParams(dimension_semantics=sem, vmem_limit_bytes=VMEM_LIMIT_BYTES)


def _rms_rows(x, g):
    ms = jnp.mean(x * x, axis=-1, keepdims=True)
    return x * lax.rsqrt(ms + RMS_EPS) * g


def _rms_to_scratch(x_ref, g_ref, h_ref, rows):
    g = g_ref[...]

    def body(c, carry):
        r0 = pl.multiple_of(c * NORM_CHUNK, NORM_CHUNK)
        h_ref[pl.ds(r0, NORM_CHUNK), :] = _rms_rows(x_ref[pl.ds(r0, NORM_CHUNK), :], g).astype(h_ref.dtype)
        return carry

    lax.fori_loop(0, rows // NORM_CHUNK, body, 0)


def _in_proj_kernel(x_ref, g_ref, w_ref, u_ref, qkv_ref, gate_ref, h_ref, *, n_pool, n_q, n_qkv, q_scale):
    j = pl.program_id(1)

    @pl.when(j == 0)
    def _():
        _rms_to_scratch(x_ref, g_ref, h_ref, x_ref.shape[0])

    def project():
        return jnp.dot(h_ref[...], w_ref[...].astype(BF16), preferred_element_type=F32)

    @pl.when(j < n_pool)
    def _():
        u_ref[...] = project()

    @pl.when((j >= n_pool) & (j < n_pool + n_q))
    def _():
        qkv_ref[...] = (project() * q_scale).astype(BF16)

    @pl.when((j >= n_pool + n_q) & (j < n_pool + n_qkv))
    def _():
        qkv_ref[...] = project().astype(BF16)

    @pl.when(j >= n_pool + n_qkv)
    def _():
        gate_ref[...] = 1.0 / (1.0 + jnp.exp(-project()))


def _in_proj(x, g, w, layer, pool_w, sb_w):
    n, d = x.shape
    cols = w.shape[2]
    gate_w = cols - pool_w - 3 * sb_w
    tm, tn = IN_TM, IN_TN
    n_pool, n_q, n_qkv, n_gate = pool_w // tn, sb_w // tn, 3 * sb_w // tn, gate_w // tn
    kern = functools.partial(_in_proj_kernel, n_pool=n_pool, n_q=n_q, n_qkv=n_qkv,
                             q_scale=1.0 / math.sqrt(SB_HEAD_DIM))
    return pl.pallas_call(
        kern,
        grid=(n // tm, cols // tn),
        in_specs=[
            pl.BlockSpec((tm, d), lambda i, j: (i, 0)),
            pl.BlockSpec((1, d), lambda i, j: (0, 0)),
            pl.BlockSpec((None, d, tn), lambda i, j: (layer, 0, j)),
        ],
        out_specs=[
            pl.BlockSpec((tm, tn), lambda i, j: (i, jnp.minimum(j, n_pool - 1))),
            pl.BlockSpec((tm, tn), lambda i, j: (i, jnp.clip(j - n_pool, 0, n_qkv - 1))),
            pl.BlockSpec((tm, tn), lambda i, j: (i, jnp.maximum(j - n_pool - n_qkv, 0))),
        ],
        out_shape=[
            jax.ShapeDtypeStruct((n, pool_w), F32),
            jax.ShapeDtypeStruct((n, 3 * sb_w), BF16),
            jax.ShapeDtypeStruct((n, gate_w), F32),
        ],
        scratch_shapes=[pltpu.VMEM((tm, d), BF16)],
        compiler_params=_params(("arbitrary", "arbitrary")),
        name="in_proj",
    )(x, g, w)


def _attn_kernel(*refs, n_cast):
    q_ref, k_ref, v_ref = refs[:3]
    cast_in = refs[3:3 + n_cast]
    o_ref = refs[3 + n_cast]
    cast_out = refs[4 + n_cast:4 + 2 * n_cast]
    acc_ref, z0_ref, z1_ref, carry_ref = refs[4 + 2 * n_cast:]
    for src, dst in zip(cast_in, cast_out):
        dst[...] = src[...].astype(dst.dtype)

    t, dh = ATT_T, SB_HEAD_DIM
    qi = pl.program_id(2)

    row = lax.broadcasted_iota(jnp.int32, (t, t), 0)
    col = lax.broadcasted_iota(jnp.int32, (t, t), 1)
    tri = (row >= col).astype(BF16)
    causal = col < row

    heads = range(ATT_HEADS)
    cols = [slice(hh * dh, (hh + 1) * dh) for hh in heads]

    z_refs = (z0_ref, z1_ref)

    def scores_into(kb, slot):
        k0 = pl.multiple_of(kb * t, t)
        for hh in heads:
            z_refs[slot][hh] = lax.dot_general(q_ref[:, cols[hh]], k_ref[pl.ds(k0, t), cols[hh]],
                                               (((1,), (1,)), ((), ())), preferred_element_type=F32)

    def block(kb, slot, diag=False, ahead=True):
        if ahead:
            scores_into(jnp.maximum(kb - 1, 0), 1 - slot)
        k0 = pl.multiple_of(kb * t, t)
        suf = []
        for hh in heads:
            z = z_refs[slot][hh]
            sp = jnp.maximum(z, 0.0) + jnp.log(1.0 + jnp.exp2(jnp.abs(z) * -LOG2E))
            if diag:
                sp = jnp.where(causal, sp, 0.0)
            suf.append(jnp.dot(sp.astype(BF16), tri, preferred_element_type=F32))
        for hh in heads:
            z = z_refs[slot][hh]
            if diag:
                a = jnp.where(causal, jnp.exp(z - suf[hh]), 0.0)
            else:
                a = jnp.exp(z - suf[hh] - carry_ref[hh])
            pv = jnp.dot(a.astype(BF16), v_ref[pl.ds(k0, t), cols[hh]], preferred_element_type=F32)
            if diag:
                acc_ref[hh] = pv
                carry_ref[hh] = suf[hh][:, 0:1]
            else:
                acc_ref[hh] += pv
                carry_ref[hh] += suf[hh][:, 0:1]

    scores_into(qi, 0)
    block(qi, 0, diag=True)

    def pair(p, carry):
        kb = qi - 1 - 2 * p
        block(kb, 1)
        block(kb - 1, 0)
        return carry

    lax.fori_loop(0, qi // 2, pair, 0)

    @pl.when(qi % 2 == 1)
    def _():
        block(0, 1, ahead=False)

    for hh in heads:
        o_ref[:, cols[hh]] = acc_ref[hh].astype(o_ref.dtype)


def _attention(qkv, batch, seq, cast=()):
    n = qkv.shape[0]
    t, h, dh = ATT_T, SB_HEADS, SB_HEAD_DIM
    nq = seq // t
    hg = h // ATT_HEADS
    w = ATT_HEADS * dh
    steps = batch * hg * nq
    cast_in, cast_out, cast_shapes = [], [], []
    for a, first_row, n_rows in cast:
        rows = n_rows // steps
        assert rows * steps == n_rows and rows % 16 == 0 and first_row % rows == 0, (a.shape, first_row, n_rows)
        first = first_row // rows
        cast_in.append(pl.BlockSpec((rows, a.shape[1]),
                                    lambda b, g, i, first=first: (first + (b * hg + g) * nq + i, 0)))
        cast_out.append(pl.BlockSpec((rows, a.shape[1]), lambda b, g, i: ((b * hg + g) * nq + i, 0)))
        cast_shapes.append(jax.ShapeDtypeStruct((n_rows, a.shape[1]), BF16))
    outs = pl.pallas_call(
        functools.partial(_attn_kernel, n_cast=len(cast)),
        grid=(batch, hg, nq),
        in_specs=[
            pl.BlockSpec((t, w), lambda b, g, i: (b * nq + i, g)),
            pl.BlockSpec((seq, w), lambda b, g, i: (b, hg + g)),
            pl.BlockSpec((seq, w), lambda b, g, i: (b, 2 * hg + g)),
        ] + cast_in,
        out_specs=[pl.BlockSpec((t, w), lambda b, g, i: (b * nq + i, g))] + cast_out,
        out_shape=[jax.ShapeDtypeStruct((n, h * dh), BF16)] + cast_shapes,
        scratch_shapes=[
            pltpu.VMEM((ATT_HEADS, t, dh), F32),
            pltpu.VMEM((ATT_HEADS, t, t), F32),
            pltpu.VMEM((ATT_HEADS, t, t), F32),
            pltpu.VMEM((ATT_HEADS, t, 1), F32),
        ],
        compiler_params=_params(("arbitrary", "arbitrary", "arbitrary")),
        name="sb_attention",
    )(qkv, qkv, qkv, *(a for a, _, _ in cast))
    return outs[0], tuple(outs[1:])


def _mix_out_kernel(u_ref, up_ref, ya_ref, gate_ref, x_ref, pw_ref, ps_ref, wpu_ref, wau_ref, wo_ref,
                    o_ref, *, seq):
    tm, pool_w = u_ref.shape
    d = x_ref.shape[1]
    gd = pool_w // len(POOL_WINDOWS)
    i = pl.program_id(0)
    row0 = (i * tm) % seq

    u = u_ref[...]
    halo = jnp.where(row0 > 0, up_ref[...], 0.0)
    pos = (row0 + lax.broadcasted_iota(jnp.int32, (tm, 1), 0) + 1).astype(F32)

    mixed = []
    for gi, w in enumerate(POOL_WINDOWS):
        c0 = gi * gd
        s = jnp.concatenate([halo[:, c0:c0 + gd], u[:, c0:c0 + gd]], axis=0)
        step = 1
        while step < w:
            s = s + pltpu.roll(s, step, axis=0)
            step *= 2
        pooled = s[POOL_HALO:, :] / jnp.minimum(pos, float(w)) - u[:, c0:c0 + gd]
        mixed.append(jnp.dot(pooled.astype(BF16), pw_ref[gi], preferred_element_type=F32))
    y_pool = (jnp.concatenate(mixed, axis=1) * ps_ref[...]).astype(BF16)

    p_up = jnp.dot(y_pool, wpu_ref[...], preferred_element_type=F32)
    a_up = jnp.dot(ya_ref[...], wau_ref[...], preferred_element_type=F32)
    merged = gate_ref[:, :d] * p_up + gate_ref[:, d:] * a_up
    o_ref[...] = x_ref[...] + jnp.dot(merged.astype(BF16), wo_ref[...], preferred_element_type=F32)


def _layer_spec(stacked_shape, layer):
    rest = tuple(stacked_shape[1:])
    return pl.BlockSpec((None,) + rest, lambda i: (layer,) + (0,) * len(rest), pipeline_mode=pl.Buffered(1))


def _mix_out(u, y_attn, gates, x, pool_w, pool_scale, w_pool_up, w_attn_up, w_out, layer, seq):
    n, d = x.shape
    pw = u.shape[1]
    tm = MIX_TM
    hb = tm // POOL_HALO
    return pl.pallas_call(
        functools.partial(_mix_out_kernel, seq=seq),
        grid=(n // tm,),
        in_specs=[
            pl.BlockSpec((tm, pw), lambda i: (i, 0)),
            pl.BlockSpec((POOL_HALO, pw), lambda i: (jnp.maximum(i * hb - 1, 0), 0)),
            pl.BlockSpec((tm, y_attn.shape[1]), lambda i: (i, 0)),
            pl.BlockSpec((tm, 2 * d), lambda i: (i, 0)),
            pl.BlockSpec((tm, d), lambda i: (i, 0)),
            _layer_spec(pool_w.shape, layer),
            _layer_spec(pool_scale.shape, layer),
            _layer_spec(w_pool_up.shape, layer),
            _layer_spec(w_attn_up.shape, layer),
            _layer_spec(w_out.shape, layer),
        ],
        out_specs=pl.BlockSpec((tm, d), lambda i: (i, 0)),
        out_shape=jax.ShapeDtypeStruct((n, d), F32),
        compiler_params=_params(("arbitrary",)),
        name="mix_out",
    )(u, u, y_attn, gates, x, pool_w, pool_scale, w_pool_up, w_attn_up, w_out)


def _silu(x):
    return x / (1.0 + jnp.exp(-x))


def _ffn_kernel(x_ref, g_ref, wg_ref, wu_ref, wd_ref, gf_ref, o_ref, h_ref, *, final_norm):
    j = pl.program_id(1)

    @pl.when(j == 0)
    def _():
        _rms_to_scratch(x_ref, g_ref, h_ref, x_ref.shape[0])
        o_ref[...] = x_ref[...]

    h = h_ref[...]
    act = _silu(jnp.dot(h, wg_ref[...], preferred_element_type=F32)) * jnp.dot(
        h, wu_ref[...], preferred_element_type=F32)
    o_ref[...] += jnp.dot(act.astype(BF16), wd_ref[...], preferred_element_type=F32)

    if final_norm:
        @pl.when(j == pl.num_programs(1) - 1)
        def _():
            o_ref[...] = _rms_rows(o_ref[...], gf_ref[...])


def _ffn(x, g, w_gate, w_up, w_down, layer, g_final, final_norm):
    n, d = x.shape
    dff = w_gate.shape[2]
    tm, fc = FFN_TM, FFN_FC
    return pl.pallas_call(
        functools.partial(_ffn_kernel, final_norm=final_norm),
        grid=(n // tm, dff // fc),
        in_specs=[
            pl.BlockSpec((tm, d), lambda i, j: (i, 0)),
            pl.BlockSpec((1, d), lambda i, j: (0, 0)),
            pl.BlockSpec((None, d, fc), lambda i, j: (layer, 0, j)),
            pl.BlockSpec((None, d, fc), lambda i, j: (layer, 0, j)),
            pl.BlockSpec((None, fc, d), lambda i, j: (layer, j, 0)),
            pl.BlockSpec((1, d), lambda i, j: (0, 0)),
        ],
        out_specs=pl.BlockSpec((tm, d), lambda i, j: (i, 0)),
        out_shape=jax.ShapeDtypeStruct((n, d), F32),
        scratch_shapes=[pltpu.VMEM((tm, d), BF16)],
        compiler_params=_params(("arbitrary", "arbitrary")),
        name="ffn_dense",
    )(x, g, w_gate, w_up, w_down, g_final)


def _router_kernel(x_ref, g_ref, wr_ref, h_ref, idx_ref, wt_ref, rank_ref, count_ref, cnt_ref, before_ref):
    h = _rms_rows(x_ref[...], g_ref[...])
    h_ref[...] = h
    h_hi = h.astype(BF16)
    h_lo = (h - h_hi.astype(F32)).astype(BF16)
    wr = wr_ref[...]
    w_hi = wr.astype(BF16)
    w_lo = (wr - w_hi.astype(F32)).astype(BF16)
    nt = (((1,), (1,)), ((), ()))
    logits = (lax.dot_general(w_hi, h_hi, nt, preferred_element_type=F32)
              + lax.dot_general(w_lo, h_hi, nt, preferred_element_type=F32)
              + lax.dot_general(w_hi, h_lo, nt, preferred_element_type=F32))

    ne = logits.shape[0]
    eid = lax.broadcasted_iota(jnp.int32, logits.shape, 0)
    m1 = jnp.max(logits, axis=0, keepdims=True)
    i1 = jnp.min(jnp.where(logits == m1, eid, ne), axis=0, keepdims=True)
    rest = jnp.where(eid == i1, -jnp.inf, logits)
    m2 = jnp.max(rest, axis=0, keepdims=True)
    i2 = jnp.min(jnp.where(rest == m2, eid, ne), axis=0, keepdims=True)
    e = jnp.exp(m2 - m1)
    idx_ref[0:1, :] = i1
    idx_ref[1:2, :] = i2
    wt_ref[0:1, :] = 1.0 / (1.0 + e)
    wt_ref[1:2, :] = e / (1.0 + e)

    tm = logits.shape[1]

    @pl.when(pl.program_id(0) == 0)
    def _():
        cnt_ref[...] = jnp.zeros_like(cnt_ref)
        r = lax.broadcasted_iota(jnp.int32, (tm, tm), 0)
        c = lax.broadcasted_iota(jnp.int32, (tm, tm), 1)
        before_ref[...] = (r < c).astype(BF16)

    oh1 = (eid == i1).astype(F32)
    oh2 = (eid == i2).astype(F32)
    pre1 = jnp.dot(oh1.astype(BF16), before_ref[...], preferred_element_type=F32)
    pre2 = jnp.dot(oh2.astype(BF16), before_ref[...], preferred_element_type=F32)
    tot1 = jnp.sum(oh1, axis=1, keepdims=True)
    tot2 = jnp.sum(oh2, axis=1, keepdims=True)
    base = cnt_ref[:, 0:1]
    rank_ref[0:1, :] = jnp.sum(oh1 * (pre1 + base), axis=0, keepdims=True).astype(jnp.int32)
    rank_ref[1:2, :] = jnp.sum(oh2 * (pre2 + base + tot1), axis=0, keepdims=True).astype(jnp.int32)
    cnt_ref[...] = cnt_ref[...] + (tot1 + tot2)
    count_ref[...] = cnt_ref[...].astype(jnp.int32)


def _router(x, g, w_router_t):
    n, d = x.shape
    tm = RT_TM
    ne = w_router_t.shape[0]
    lanes = 128
    return pl.pallas_call(
        _router_kernel,
        grid=(n // tm,),
        in_specs=[
            pl.BlockSpec((tm, d), lambda i: (i, 0)),
            pl.BlockSpec((1, d), lambda i: (0, 0)),
            pl.BlockSpec(w_router_t.shape, lambda i: (0, 0)),
        ],
        out_specs=[
            pl.BlockSpec((tm, d), lambda i: (i, 0)),
            pl.BlockSpec((TOP_K, tm), lambda i: (0, i)),
            pl.BlockSpec((TOP_K, tm), lambda i: (0, i)),
            pl.BlockSpec((TOP_K, tm), lambda i: (0, i)),
            pl.BlockSpec((ne, lanes), lambda i: (0, 0)),
        ],
        out_shape=[
            jax.ShapeDtypeStruct((n, d), F32),
            jax.ShapeDtypeStruct((TOP_K, n), jnp.int32),
            jax.ShapeDtypeStruct((TOP_K, n), F32),
            jax.ShapeDtypeStruct((TOP_K, n), jnp.int32),
            jax.ShapeDtypeStruct((ne, lanes), jnp.int32),
        ],
        scratch_shapes=[pltpu.VMEM((ne, lanes), F32), pltpu.VMEM((tm, tm), BF16)],
        compiler_params=_params(("arbitrary",)),
        name="moe_router",
    )(x, g, w_router_t)


ROW_COPY_UNROLL = 8


def _moe_kernel(te_ref, nr_ref, tok_ref, dest_ref, hsrc_ref, wg_ref, wul_ref, wuh_ref, wd_ref, out_ref,
                gbuf, hbf, ybuf, gsem, sem, *, ne_lo, rows_per_step):
    t = pl.program_id(0)
    j = pl.program_id(1)
    nt = pl.num_programs(0)
    nj = pl.num_programs(1)
    tm = gbuf.shape[1]
    n_out = out_ref.shape[0] - 2 * tm
    s = t % 2
    valid = nr_ref[t] > 0

    def fetch_row(tile, slot, r):
        return pltpu.make_async_copy(hsrc_ref.at[pl.ds(tok_ref[tile * tm + r], 1), :],
                                     gbuf.at[slot, pl.ds(r, 1), :], gsem.at[slot])

    def write_row(tile, slot, r):
        return pltpu.make_async_copy(ybuf.at[slot, pl.ds(r, 1), :],
                                     out_ref.at[pl.ds(dest_ref[(tile + 1) * tm + r], 1), :], sem.at[slot])

    def wait_fetch(slot):
        pltpu.make_async_copy(hsrc_ref.at[pl.ds(0, tm), :], gbuf.at[slot], gsem.at[slot]).wait()

    def wait_write(slot):
        pltpu.make_async_copy(ybuf.at[slot], out_ref.at[pl.ds(0, tm), :], sem.at[slot]).wait()

    def for_all_rows(fn):
        def group(g, carry):
            for u in range(ROW_COPY_UNROLL):
                fn(g * ROW_COPY_UNROLL + u)
            return carry

        lax.fori_loop(0, tm // ROW_COPY_UNROLL, group, 0)

    @pl.when(j == 0)
    def _():
        @pl.when(t == 0)
        def _():
            ybuf[...] = jnp.zeros(ybuf.shape, F32)
            for slot in range(2):
                tail = pltpu.make_async_copy(ybuf.at[slot], out_ref.at[pl.ds(n_out + slot * tm, tm), :],
                                             sem.at[slot])
                tail.start()
                tail.wait()
            for_all_rows(lambda r: fetch_row(0, 0, r).start())

        prev_valid = nr_ref[jnp.maximum(t - 1, 0)] > 0

        @pl.when((t == 1) | ((t >= 2) & (nr_ref[jnp.maximum(t - 2, 0)] > 0)))
        def _():
            wait_write(s)

        @pl.when((t == 0) | prev_valid)
        def _():
            wait_fetch(s)

        @pl.when(valid)
        def _():
            hbf[...] = gbuf[s].astype(BF16)
            ybuf[s] = jnp.zeros(ybuf.shape[1:], F32)

    @pl.when(valid)
    def _():
        for u in range(rows_per_step):
            r = j * rows_per_step + u
            fetch_row(t + 1, 1 - s, r).start()
            write_row(t - 1, 1 - s, r).start()
        h = hbf[...]
        w_up = jnp.where(te_ref[t] < ne_lo, wul_ref[...], wuh_ref[...])
        act = _silu(jnp.dot(h, wg_ref[...], preferred_element_type=F32)) * jnp.dot(
            h, w_up, preferred_element_type=F32)
        ybuf[s] += jnp.dot(act.astype(BF16), wd_ref[...], preferred_element_type=F32)

    @pl.when(j == nj - 1)
    def _():
        is_last = t == nt - 1

        @pl.when(valid & (is_last | (nr_ref[jnp.minimum(t + 1, nt - 1)] == 0)))
        def _():
            for_all_rows(lambda r: write_row(t, s, r).start())

        @pl.when(is_last)
        def _():
            @pl.when(nr_ref[nt - 2] > 0)
            def _():
                wait_write(1 - s)

            @pl.when(valid)
            def _():
                wait_write(s)
                wait_fetch(1 - s)


def _moe_gmm(tile_expert, tile_rows, token_of_slot, dest, h, w_gate, w_up_lo, w_up_hi, w_down):
    n, d = h.shape
    tm, fc = MOE_TM, MOE_FC
    n_tiles = tile_rows.shape[0]
    dff = w_gate.shape[2]
    ne_lo = w_up_lo.shape[0]
    nfc = dff // fc
    rows_per_step = tm // nfc
    assert rows_per_step * nfc == tm and tm % ROW_COPY_UNROLL == 0 and n_tiles >= 2
    assert token_of_slot.shape[0] == dest.shape[0] == (n_tiles + 1) * tm

    def chunk(t, j, nr):
        return jnp.where(nr[t] > 0, j, nfc - 1)

    def up_lo_block(t, j, te, nr, tk, ds):
        lo = te[t] < ne_lo
        return (jnp.minimum(te[t], ne_lo - 1), 0, jnp.where(lo, chunk(t, j, nr), nfc - 1))

    def up_hi_block(t, j, te, nr, tk, ds):
        hi = te[t] >= ne_lo
        return (jnp.maximum(te[t] - ne_lo, 0), 0, jnp.where(hi, chunk(t, j, nr), 0))

    grid_spec = pltpu.PrefetchScalarGridSpec(
        num_scalar_prefetch=4,
        grid=(n_tiles, nfc),
        in_specs=[
            pl.BlockSpec(memory_space=pl.ANY),
            pl.BlockSpec((None, d, fc), lambda t, j, te, nr, tk, ds: (te[t], 0, chunk(t, j, nr))),
            pl.BlockSpec((None, d, fc), up_lo_block),
            pl.BlockSpec((None, d, fc), up_hi_block),
            pl.BlockSpec((None, fc, d), lambda t, j, te, nr, tk, ds: (te[t], chunk(t, j, nr), 0)),
        ],
        out_specs=pl.BlockSpec(memory_space=pl.ANY),
        scratch_shapes=[
            pltpu.VMEM((2, tm, d), F32),
            pltpu.VMEM((tm, d), BF16),
            pltpu.VMEM((2, tm, d), F32),
            pltpu.SemaphoreType.DMA((2,)),
            pltpu.SemaphoreType.DMA((2,)),
        ],
    )
    return pl.pallas_call(
        functools.partial(_moe_kernel, ne_lo=ne_lo, rows_per_step=rows_per_step),
        grid_spec=grid_spec,
        out_shape=jax.ShapeDtypeStruct((TOP_K * n + 2 * tm, d), F32),
        compiler_params=_params(("arbitrary", "arbitrary")),
        name="moe_gmm",
    )(tile_expert, tile_rows, token_of_slot, dest, h, w_gate, w_up_lo, w_up_hi, w_down)


def _moe_plan(idx, ranks, counts, n_tokens):
    tm = MOE_TM
    n_assign = TOP_K * n_tokens
    n_tiles = -(-n_assign // tm) + N_EXPERTS
    n_slots = n_tiles * tm
    padded = ((counts + tm - 1) // tm) * tm
    ends = jnp.cumsum(padded)
    offsets = ends - padded
    experts = jnp.arange(N_EXPERTS, dtype=jnp.int32)
    slot = (jnp.sum(jnp.where(idx[..., None] == experts, offsets, 0), axis=-1) + ranks).reshape(-1)
    assign_of_slot = jnp.full((n_slots,), -1, jnp.int32).at[slot].set(jnp.arange(n_assign, dtype=jnp.int32))
    is_pad = assign_of_slot < 0
    token_of_slot = jnp.where(is_pad, 0, assign_of_slot % n_tokens)
    token_of_slot = jnp.concatenate([token_of_slot, jnp.zeros((tm,), jnp.int32)])
    p = jnp.arange(-tm, n_slots, dtype=jnp.int32)
    tail_row = n_assign + ((p // tm) % 2) * tm + p % tm
    dest = jnp.where(jnp.concatenate([jnp.ones((tm,), bool), is_pad]), tail_row,
                     jnp.concatenate([jnp.zeros((tm,), jnp.int32), assign_of_slot]))
    tile_start = jnp.arange(n_tiles, dtype=jnp.int32) * tm
    tile_used = tile_start < ends[-1]
    tile_expert = jnp.sum((tile_start[:, None] >= ends[None, :]).astype(jnp.int32), axis=1)
    last_expert = jnp.sum((ends[-1] - 1 >= ends).astype(jnp.int32))
    tile_expert = jnp.where(tile_used, tile_expert, last_expert).astype(jnp.int32)
    group_end = jnp.sum(jnp.where(tile_expert[:, None] == experts, offsets + counts, 0), axis=1)
    tile_rows = jnp.where(tile_used, jnp.clip(group_end - tile_start, 0, tm), 0).astype(jnp.int32)
    return token_of_slot, dest, tile_expert, tile_rows


def _combine_kernel(x_ref, y1_ref, y2_ref, w_ref, gf_ref, o_ref, *, final_norm):
    w = w_ref[...]
    out = x_ref[...] + w[:, 0:1] * y1_ref[...] + w[:, 1:2] * y2_ref[...]
    if final_norm:
        out = _rms_rows(out, gf_ref[...])
    o_ref[...] = out


def _combine(x, yg, wts_t, g_final, final_norm):
    n, d = x.shape
    tm = CMB_TM
    nb = n // tm
    row = pl.BlockSpec((tm, d), lambda i: (i, 0))
    return pl.pallas_call(
        functools.partial(_combine_kernel, final_norm=final_norm),
        grid=(nb,),
        in_specs=[row, row, pl.BlockSpec((tm, d), lambda i: (i + nb, 0)),
                  pl.BlockSpec((tm, TOP_K), lambda i: (i, 0)), pl.BlockSpec((1, d), lambda i: (0, 0))],
        out_specs=row,
        out_shape=jax.ShapeDtypeStruct((n, d), F32),
        compiler_params=_params(("arbitrary",)),
        name="moe_combine",
    )(x, yg, yg, wts_t, g_final)


def _moe_layer(x, g, w_router, w_gate, w_up_lo, w_up_hi, w_down, g_final, final_norm):
    n, d = x.shape
    h, idx, wts, ranks, counts = _router(x, g, w_router.T)
    token_of_slot, dest, tile_expert, tile_rows = _moe_plan(idx, ranks, counts[:, 0], n)
    yg = _moe_gmm(tile_expert, tile_rows, token_of_slot, dest, h, w_gate, w_up_lo, w_up_hi, w_down)
    return _combine(x, yg, wts.T, g_final, final_norm)


def kernel(x, norm_mix, w_in, pool_w, pool_scale, w_pool_up, w_attn_up, w_out, norm_ffn, ffn_gate,
           ffn_up, ffn_down, w_router, moe_gate, moe_up, moe_down, norm_final):
    batch, seq, d = x.shape
    depth = norm_mix.shape[0]
    pool_width = pool_scale.shape[1]
    sb_width = w_attn_up.shape[1]
    n = batch * seq
    xf = x.reshape(n, d)
    g_final = norm_final.reshape(1, d)

    pool_w_b, w_pool_up_b, w_attn_up_b, w_out_b, ffn_gate_b, ffn_up_b, ffn_down_b = (
        a.astype(BF16) for a in (pool_w, w_pool_up, w_attn_up, w_out, ffn_gate, ffn_up, ffn_down))
    pool_scale_r = pool_scale.reshape(depth, 1, pool_width)
    n_exp, _, dff = moe_gate.shape[1:]
    ne_lo = n_exp // 2
    moe_gate2, moe_up2, moe_down2 = (a.reshape(-1, a.shape[-1]) for a in (moe_gate, moe_up, moe_down))

    moe_w = None
    for i in range(depth):
        last = i == depth - 1
        j = i // 2
        u, qkv, gates = _in_proj(xf, norm_mix[i].reshape(1, d), w_in, i, pool_width, sb_width)
        if 2 * j + 1 >= depth:
            cast = ()
        elif i % 2 == 0:
            cast = ((moe_gate2, j * n_exp * d, n_exp * d), (moe_up2, j * n_exp * d, ne_lo * d))
        else:
            cast = ((moe_down2, j * n_exp * dff, n_exp * dff),
                    (moe_up2, (j * n_exp + ne_lo) * d, (n_exp - ne_lo) * d))
        y_attn, cast_b = _attention(qkv, batch, seq, cast)
        xf = _mix_out(u, y_attn, gates, xf, pool_w_b, pool_scale_r, w_pool_up_b, w_attn_up_b, w_out_b, i, seq)
        gf = norm_ffn[i].reshape(1, d)
        if i % 2 == 0:
            if cast_b:
                moe_w = (cast_b[0].reshape(n_exp, d, dff), cast_b[1].reshape(ne_lo, d, dff))
            xf = _ffn(xf, gf, ffn_gate_b, ffn_up_b, ffn_down_b, j, g_final, last)
        else:
            w_gate_b, w_up_lo_b = moe_w
            w_down_b, w_up_hi_b = cast_b[0].reshape(n_exp, dff, d), cast_b[1].reshape(n_exp - ne_lo, d, dff)
            xf = _moe_layer(xf, gf, w_router[j], w_gate_b, w_up_lo_b, w_up_hi_b, w_down_b, g_final, last)
    return xf.reshape(batch, seq, d)
```

```python
import functools
import math

import jax
import jax.numpy as jnp
from jax import lax
from jax.experimental import pallas as pl
from jax.experimental.pallas import tpu as pltpu

F32 = jnp.float32
BF16 = jnp.bfloat16

RMS_EPS = 1e-6
LOG2E = 1.0 / math.log(2.0)
POOL_WINDOWS = (2, 4, 8, 16)
POOL_HALO = 16
SB_HEADS = 8
SB_HEAD_DIM = 128
N_EXPERTS = 8
TOP_K = 2

VMEM_LIMIT_BYTES = 56 * 1024 * 1024
IN_PROJ_VMEM_LIMIT_BYTES = 60 * 1024 * 1024

IN_TM, IN_TN = 1024, 1024
ATT_T = 256
ATT_HEADS = 4
MIX_TM = 256
FFN_TM, FFN_FC = 1024, 512
RT_TM = 512
MOE_TM, MOE_FC = 528, 512
CMB_TM = 512
NORM_CHUNK = 128


def _params(sem, vmem_limit_bytes=VMEM_LIMIT_BYTES):
    return pltpu.CompilerParams(dimension_semantics=sem, vmem_limit_bytes=vmem_limit_bytes)


def _rms_rows(x, g):
    ms = jnp.mean(x * x, axis=-1, keepdims=True)
    return x * lax.rsqrt(ms + RMS_EPS) * g


def _rms_to_scratch(x_ref, g_ref, h_ref, rows):
    g = g_ref[...]

    def body(c, carry):
        r0 = pl.multiple_of(c * NORM_CHUNK, NORM_CHUNK)
        h_ref[pl.ds(r0, NORM_CHUNK), :] = _rms_rows(x_ref[pl.ds(r0, NORM_CHUNK), :], g).astype(h_ref.dtype)
        return carry

    lax.fori_loop(0, rows // NORM_CHUNK, body, 0)


def _in_proj_kernel(x_ref, g_ref, w_ref, u_ref, qkv_ref, gate_ref, h_ref, *, n_pool, n_q, n_qkv, q_scale):
    j = pl.program_id(1)

    @pl.when(j == 0)
    def _():
        _rms_to_scratch(x_ref, g_ref, h_ref, x_ref.shape[0])

    def project():
        return jnp.dot(h_ref[...], w_ref[...].astype(BF16), preferred_element_type=F32)

    @pl.when(j < n_pool)
    def _():
        u_ref[...] = project()

    @pl.when((j >= n_pool) & (j < n_pool + n_q))
    def _():
        qkv_ref[...] = (project() * q_scale).astype(BF16)

    @pl.when((j >= n_pool + n_q) & (j < n_pool + n_qkv))
    def _():
        qkv_ref[...] = project().astype(BF16)

    @pl.when(j >= n_pool + n_qkv)
    def _():
        gate_ref[...] = (1.0 / (1.0 + jnp.exp(-project()))).astype(gate_ref.dtype)


def _in_proj(x, g, w, layer, pool_w, sb_w):
    n, d = x.shape
    cols = w.shape[2]
    gate_w = cols - pool_w - 3 * sb_w
    tm, tn = IN_TM, IN_TN
    n_pool, n_q, n_qkv, n_gate = pool_w // tn, sb_w // tn, 3 * sb_w // tn, gate_w // tn
    kern = functools.partial(_in_proj_kernel, n_pool=n_pool, n_q=n_q, n_qkv=n_qkv,
                             q_scale=1.0 / math.sqrt(SB_HEAD_DIM))
    return pl.pallas_call(
        kern,
        grid=(n // tm, cols // tn),
        in_specs=[
            pl.BlockSpec((tm, d), lambda i, j: (i, 0)),
            pl.BlockSpec((1, d), lambda i, j: (0, 0)),
            pl.BlockSpec((None, d, tn), lambda i, j: (layer, 0, j)),
        ],
        out_specs=[
            pl.BlockSpec((tm, tn), lambda i, j: (i, jnp.minimum(j, n_pool - 1))),
            pl.BlockSpec((tm, tn), lambda i, j: (i, jnp.clip(j - n_pool, 0, n_qkv - 1))),
            pl.BlockSpec((tm, tn), lambda i, j: (i, jnp.maximum(j - n_pool - n_qkv, 0))),
        ],
        out_shape=[
            jax.ShapeDtypeStruct((n, pool_w), F32),
            jax.ShapeDtypeStruct((n, 3 * sb_w), BF16),
            jax.ShapeDtypeStruct((n, gate_w), BF16),
        ],
        scratch_shapes=[pltpu.VMEM((tm, d), BF16)],
        compiler_params=_params(("arbitrary", "arbitrary"), IN_PROJ_VMEM_LIMIT_BYTES),
        name="in_proj",
    )(x, g, w)


def _attn_kernel(*refs, n_cast):
    q_ref, k_ref, v_ref = refs[:3]
    cast_in = refs[3:3 + n_cast]
    o_ref = refs[3 + n_cast]
    cast_out = refs[4 + n_cast:4 + 2 * n_cast]
    acc_ref, z0_ref, z1_ref, carry_ref = refs[4 + 2 * n_cast:]

    t, dh = ATT_T, SB_HEAD_DIM
    qi = pl.program_id(2)

    row = lax.broadcasted_iota(jnp.int32, (t, t), 0)
    col = lax.broadcasted_iota(jnp.int32, (t, t), 1)
    tri = (row >= col).astype(BF16)
    causal = col < row

    heads = range(ATT_HEADS)
    cols = [slice(hh * dh, (hh + 1) * dh) for hh in heads]

    z_refs = (z0_ref, z1_ref)

    def scores_into(kb, slot):
        k0 = pl.multiple_of(kb * t, t)
        for hh in heads:
            z_refs[slot][hh] = lax.dot_general(q_ref[:, cols[hh]], k_ref[pl.ds(k0, t), cols[hh]],
                                               (((1,), (1,)), ((), ())), preferred_element_type=F32)

    def block(kb, slot, diag=False, ahead=True):
        if ahead:
            scores_into(jnp.maximum(kb - 1, 0), 1 - slot)
        k0 = pl.multiple_of(kb * t, t)
        suf = []
        for hh in heads:
            z = z_refs[slot][hh]
            sp = jnp.maximum(z, 0.0) + jnp.log(1.0 + jnp.exp2(jnp.abs(z) * -LOG2E))
            if diag:
                sp = jnp.where(causal, sp, 0.0)
            suf.append(jnp.dot(sp.astype(BF16), tri, preferred_element_type=F32))
        for hh in heads:
            z = z_refs[slot][hh]
            if diag:
                a = jnp.where(causal, jnp.exp(z - suf[hh]), 0.0)
            else:
                a = jnp.exp(z - suf[hh] - carry_ref[hh])
            pv = jnp.dot(a.astype(BF16), v_ref[pl.ds(k0, t), cols[hh]], preferred_element_type=F32)
            if diag:
                acc_ref[hh] = pv
                carry_ref[hh] = suf[hh][:, 0:1]
            else:
                acc_ref[hh] += pv
                carry_ref[hh] += suf[hh][:, 0:1]

    scores_into(qi, 0)
    for src, dst in zip(cast_in, cast_out):
        dst[...] = src[...].astype(dst.dtype)
    block(qi, 0, diag=True)

    def pair(p, carry):
        kb = qi - 1 - 2 * p
        block(kb, 1)
        block(kb - 1, 0)
        return carry

    lax.fori_loop(0, qi // 2, pair, 0)

    @pl.when(qi % 2 == 1)
    def _():
        block(0, 1, ahead=False)

    for hh in heads:
        o_ref[:, cols[hh]] = acc_ref[hh].astype(o_ref.dtype)


def _attention(qkv, batch, seq, cast=()):
    n = qkv.shape[0]
    t, h, dh = ATT_T, SB_HEADS, SB_HEAD_DIM
    nq = seq // t
    hg = h // ATT_HEADS
    w = ATT_HEADS * dh
    steps = batch * hg * nq
    cast_in, cast_out, cast_shapes = [], [], []
    for a, first_row, n_rows in cast:
        rows = n_rows // steps
        assert rows * steps == n_rows and rows % 16 == 0 and first_row % rows == 0, (a.shape, first_row, n_rows)
        first = first_row // rows
        cast_in.append(pl.BlockSpec((rows, a.shape[1]),
                                    lambda b, g, i, first=first: (first + (b * hg + g) * nq + i, 0)))
        cast_out.append(pl.BlockSpec((rows, a.shape[1]), lambda b, g, i: ((b * hg + g) * nq + i, 0)))
        cast_shapes.append(jax.ShapeDtypeStruct((n_rows, a.shape[1]), BF16))
    outs = pl.pallas_call(
        functools.partial(_attn_kernel, n_cast=len(cast)),
        grid=(batch, hg, nq),
        in_specs=[
            pl.BlockSpec((t, w), lambda b, g, i: (b * nq + i, g)),
            pl.BlockSpec((seq, w), lambda b, g, i: (b, hg + g)),
            pl.BlockSpec((seq, w), lambda b, g, i: (b, 2 * hg + g)),
        ] + cast_in,
        out_specs=[pl.BlockSpec((t, w), lambda b, g, i: (b * nq + i, g))] + cast_out,
        out_shape=[jax.ShapeDtypeStruct((n, h * dh), BF16)] + cast_shapes,
        scratch_shapes=[
            pltpu.VMEM((ATT_HEADS, t, dh), F32),
            pltpu.VMEM((ATT_HEADS, t, t), F32),
            pltpu.VMEM((ATT_HEADS, t, t), F32),
            pltpu.VMEM((ATT_HEADS, t, 1), F32),
        ],
        compiler_params=_params(("arbitrary", "arbitrary", "arbitrary")),
        name="sb_attention",
    )(qkv, qkv, qkv, *(a for a, _, _ in cast))
    return outs[0], tuple(outs[1:])


def _mix_out_kernel(u_ref, up_ref, ya_ref, gate_ref, x_ref, pw_ref, ps_ref, wpu_ref, wau_ref, wo_ref,
                    o_ref, *, seq):
    tm, pool_w = u_ref.shape
    d = x_ref.shape[1]
    gd = pool_w // len(POOL_WINDOWS)
    i = pl.program_id(0)
    row0 = (i * tm) % seq

    u = u_ref[...]
    halo = jnp.where(row0 > 0, up_ref[...], 0.0)
    pos = (row0 + lax.broadcasted_iota(jnp.int32, (tm, 1), 0) + 1).astype(F32)

    mixed = []
    for gi, w in enumerate(POOL_WINDOWS):
        c0 = gi * gd
        s = jnp.concatenate([halo[:, c0:c0 + gd], u[:, c0:c0 + gd]], axis=0)
        step = 1
        while step < w:
            s = s + pltpu.roll(s, step, axis=0)
            step *= 2
        pooled = s[POOL_HALO:, :] / jnp.minimum(pos, float(w)) - u[:, c0:c0 + gd]
        mixed.append(jnp.dot(pooled.astype(BF16), pw_ref[gi], preferred_element_type=F32))
    y_pool = (jnp.concatenate(mixed, axis=1) * ps_ref[...]).astype(BF16)

    p_up = jnp.dot(y_pool, wpu_ref[...], preferred_element_type=F32)
    a_up = jnp.dot(ya_ref[...], wau_ref[...], preferred_element_type=F32)
    merged = gate_ref[:, :d] * p_up + gate_ref[:, d:] * a_up
    o_ref[...] = x_ref[...] + jnp.dot(merged.astype(BF16), wo_ref[...], preferred_element_type=F32)


def _layer_spec(stacked_shape, layer):
    rest = tuple(stacked_shape[1:])
    return pl.BlockSpec((None,) + rest, lambda i: (layer,) + (0,) * len(rest), pipeline_mode=pl.Buffered(1))


def _mix_out(u, y_attn, gates, x, pool_w, pool_scale, w_pool_up, w_attn_up, w_out, layer, seq):
    n, d = x.shape
    pw = u.shape[1]
    tm = MIX_TM
    hb = tm // POOL_HALO
    return pl.pallas_call(
        functools.partial(_mix_out_kernel, seq=seq),
        grid=(n // tm,),
        in_specs=[
            pl.BlockSpec((tm, pw), lambda i: (i, 0)),
            pl.BlockSpec((POOL_HALO, pw), lambda i: (jnp.maximum(i * hb - 1, 0), 0)),
            pl.BlockSpec((tm, y_attn.shape[1]), lambda i: (i, 0)),
            pl.BlockSpec((tm, 2 * d), lambda i: (i, 0)),
            pl.BlockSpec((tm, d), lambda i: (i, 0)),
            _layer_spec(pool_w.shape, layer),
            _layer_spec(pool_scale.shape, layer),
            _layer_spec(w_pool_up.shape, layer),
            _layer_spec(w_attn_up.shape, layer),
            _layer_spec(w_out.shape, layer),
        ],
        out_specs=pl.BlockSpec((tm, d), lambda i: (i, 0)),
        out_shape=jax.ShapeDtypeStruct((n, d), F32),
        compiler_params=_params(("arbitrary",)),
        name="mix_out",
    )(u, u, y_attn, gates, x, pool_w, pool_scale, w_pool_up, w_attn_up, w_out)


def _silu(x):
    return x / (1.0 + jnp.exp(-x))


def _ffn_kernel(x_ref, g_ref, wg_ref, wu_ref, wd_ref, gf_ref, o_ref, h_ref, *, final_norm):
    j = pl.program_id(1)

    @pl.when(j == 0)
    def _():
        _rms_to_scratch(x_ref, g_ref, h_ref, x_ref.shape[0])
        o_ref[...] = x_ref[...]

    h = h_ref[...]
    act = _silu(jnp.dot(h, wg_ref[...], preferred_element_type=F32)) * jnp.dot(
        h, wu_ref[...], preferred_element_type=F32)
    o_ref[...] += jnp.dot(act.astype(BF16), wd_ref[...], preferred_element_type=F32)

    if final_norm:
        @pl.when(j == pl.num_programs(1) - 1)
        def _():
            o_ref[...] = _rms_rows(o_ref[...], gf_ref[...])


def _ffn(x, g, w_gate, w_up, w_down, layer, g_final, final_norm):
    n, d = x.shape
    dff = w_gate.shape[2]
    tm, fc = FFN_TM, FFN_FC
    return pl.pallas_call(
        functools.partial(_ffn_kernel, final_norm=final_norm),
        grid=(n // tm, dff // fc),
        in_specs=[
            pl.BlockSpec((tm, d), lambda i, j: (i, 0)),
            pl.BlockSpec((1, d), lambda i, j: (0, 0)),
            pl.BlockSpec((None, d, fc), lambda i, j: (layer, 0, j)),
            pl.BlockSpec((None, d, fc), lambda i, j: (layer, 0, j)),
            pl.BlockSpec((None, fc, d), lambda i, j: (layer, j, 0)),
            pl.BlockSpec((1, d), lambda i, j: (0, 0)),
        ],
        out_specs=pl.BlockSpec((tm, d), lambda i, j: (i, 0)),
        out_shape=jax.ShapeDtypeStruct((n, d), F32),
        scratch_shapes=[pltpu.VMEM((tm, d), BF16)],
        compiler_params=_params(("arbitrary", "arbitrary")),
        name="ffn_dense",
    )(x, g, w_gate, w_up, w_down, g_final)


def _router_kernel(x_ref, g_ref, wr_ref, h_ref, idx_ref, wt_ref, rank_ref, count_ref, cnt_ref, before_ref):
    h = _rms_rows(x_ref[...], g_ref[...])
    h_ref[...] = h
    h_hi = h.astype(BF16)
    h_lo = (h - h_hi.astype(F32)).astype(BF16)
    wr = wr_ref[...]
    w_hi = wr.astype(BF16)
    w_lo = (wr - w_hi.astype(F32)).astype(BF16)
    nt = (((1,), (1,)), ((), ()))
    logits = (lax.dot_general(w_hi, h_hi, nt, preferred_element_type=F32)
              + lax.dot_general(w_lo, h_hi, nt, preferred_element_type=F32)
              + lax.dot_general(w_hi, h_lo, nt, preferred_element_type=F32))

    ne = logits.shape[0]
    eid = lax.broadcasted_iota(jnp.int32, logits.shape, 0)
    m1 = jnp.max(logits, axis=0, keepdims=True)
    i1 = jnp.min(jnp.where(logits == m1, eid, ne), axis=0, keepdims=True)
    rest = jnp.where(eid == i1, -jnp.inf, logits)
    m2 = jnp.max(rest, axis=0, keepdims=True)
    i2 = jnp.min(jnp.where(rest == m2, eid, ne), axis=0, keepdims=True)
    e = jnp.exp(m2 - m1)
    idx_ref[0:1, :] = i1
    idx_ref[1:2, :] = i2
    wt_ref[0:1, :] = 1.0 / (1.0 + e)
    wt_ref[1:2, :] = e / (1.0 + e)

    tm = logits.shape[1]

    @pl.when(pl.program_id(0) == 0)
    def _():
        cnt_ref[...] = jnp.zeros_like(cnt_ref)
        r = lax.broadcasted_iota(jnp.int32, (tm, tm), 0)
        c = lax.broadcasted_iota(jnp.int32, (tm, tm), 1)
        before_ref[...] = (r < c).astype(BF16)

    oh1 = (eid == i1).astype(F32)
    oh2 = (eid == i2).astype(F32)
    pre1 = jnp.dot(oh1.astype(BF16), before_ref[...], preferred_element_type=F32)
    pre2 = jnp.dot(oh2.astype(BF16), before_ref[...], preferred_element_type=F32)
    tot1 = jnp.sum(oh1, axis=1, keepdims=True)
    tot2 = jnp.sum(oh2, axis=1, keepdims=True)
    base = cnt_ref[:, 0:1]
    rank_ref[0:1, :] = jnp.sum(oh1 * (pre1 + base), axis=0, keepdims=True).astype(jnp.int32)
    rank_ref[1:2, :] = jnp.sum(oh2 * (pre2 + base + tot1), axis=0, keepdims=True).astype(jnp.int32)
    cnt_ref[...] = cnt_ref[...] + (tot1 + tot2)
    count_ref[...] = cnt_ref[...].astype(jnp.int32)


def _router(x, g, w_router_t):
    n, d = x.shape
    tm = RT_TM
    ne = w_router_t.shape[0]
    lanes = 128
    return pl.pallas_call(
        _router_kernel,
        grid=(n // tm,),
        in_specs=[
            pl.BlockSpec((tm, d), lambda i: (i, 0)),
            pl.BlockSpec((1, d), lambda i: (0, 0)),
            pl.BlockSpec(w_router_t.shape, lambda i: (0, 0)),
        ],
        out_specs=[
            pl.BlockSpec((tm, d), lambda i: (i, 0)),
            pl.BlockSpec((TOP_K, tm), lambda i: (0, i)),
            pl.BlockSpec((TOP_K, tm), lambda i: (0, i)),
            pl.BlockSpec((TOP_K, tm), lambda i: (0, i)),
            pl.BlockSpec((ne, lanes), lambda i: (0, 0)),
        ],
        out_shape=[
            jax.ShapeDtypeStruct((n, d), F32),
            jax.ShapeDtypeStruct((TOP_K, n), jnp.int32),
            jax.ShapeDtypeStruct((TOP_K, n), F32),
            jax.ShapeDtypeStruct((TOP_K, n), jnp.int32),
            jax.ShapeDtypeStruct((ne, lanes), jnp.int32),
        ],
        scratch_shapes=[pltpu.VMEM((ne, lanes), F32), pltpu.VMEM((tm, tm), BF16)],
        compiler_params=_params(("arbitrary",)),
        name="moe_router",
    )(x, g, w_router_t)


ROW_COPY_UNROLL = 8


def _moe_kernel(te_ref, nr_ref, tok_ref, dest_ref, hsrc_ref, wg_ref, wul_ref, wuh_ref, wd_ref, out_ref,
                gbuf, hbf, ybuf, gsem, sem, *, ne_lo, rows_per_step):
    t = pl.program_id(0)
    j = pl.program_id(1)
    nt = pl.num_programs(0)
    nj = pl.num_programs(1)
    tm = gbuf.shape[1]
    n_out = out_ref.shape[0] - 2 * tm
    s = t % 2
    valid = nr_ref[t] > 0

    def fetch_row(tile, slot, r):
        return pltpu.make_async_copy(hsrc_ref.at[pl.ds(tok_ref[tile * tm + r], 1), :],
                                     gbuf.at[slot, pl.ds(r, 1), :], gsem.at[slot])

    def write_row(tile, slot, r):
        return pltpu.make_async_copy(ybuf.at[slot, pl.ds(r, 1), :],
                                     out_ref.at[pl.ds(dest_ref[(tile + 1) * tm + r], 1), :], sem.at[slot])

    def wait_fetch(slot):
        pltpu.make_async_copy(hsrc_ref.at[pl.ds(0, tm), :], gbuf.at[slot], gsem.at[slot]).wait()

    def wait_write(slot):
        pltpu.make_async_copy(ybuf.at[slot], out_ref.at[pl.ds(0, tm), :], sem.at[slot]).wait()

    def for_all_rows(fn):
        def group(g, carry):
            for u in range(ROW_COPY_UNROLL):
                fn(g * ROW_COPY_UNROLL + u)
            return carry

        lax.fori_loop(0, tm // ROW_COPY_UNROLL, group, 0)

    @pl.when(j == 0)
    def _():
        @pl.when(t == 0)
        def _():
            ybuf[...] = jnp.zeros(ybuf.shape, F32)
            for slot in range(2):
                tail = pltpu.make_async_copy(ybuf.at[slot], out_ref.at[pl.ds(n_out + slot * tm, tm), :],
                                             sem.at[slot])
                tail.start()
                tail.wait()
            for_all_rows(lambda r: fetch_row(0, 0, r).start())

        prev_valid = nr_ref[jnp.maximum(t - 1, 0)] > 0

        @pl.when((t == 1) | ((t >= 2) & (nr_ref[jnp.maximum(t - 2, 0)] > 0)))
        def _():
            wait_write(s)

        @pl.when((t == 0) | prev_valid)
        def _():
            wait_fetch(s)

        @pl.when(valid)
        def _():
            hbf[...] = gbuf[s].astype(BF16)
            ybuf[s] = jnp.zeros(ybuf.shape[1:], F32)

    def step(m):
        for u in range(rows_per_step):
            r = j * rows_per_step + u
            fetch_row(t + 1, 1 - s, r).start()
            write_row(t - 1, 1 - s, r).start()
        h = hbf[:m]
        w_up = jnp.where(te_ref[t] < ne_lo, wul_ref[...], wuh_ref[...])
        act = _silu(jnp.dot(h, wg_ref[...], preferred_element_type=F32)) * jnp.dot(
            h, w_up, preferred_element_type=F32)
        ybuf[s, :m] += jnp.dot(act.astype(BF16), wd_ref[...], preferred_element_type=F32)

    m1, m2 = tm // 3, 2 * tm // 3
    rows = nr_ref[t]

    @pl.when((rows > 0) & (rows <= m1))
    def _():
        step(m1)

    @pl.when((rows > m1) & (rows <= m2))
    def _():
        step(m2)

    @pl.when(rows > m2)
    def _():
        step(tm)

    @pl.when(j == nj - 1)
    def _():
        is_last = t == nt - 1

        @pl.when(valid & (is_last | (nr_ref[jnp.minimum(t + 1, nt - 1)] == 0)))
        def _():
            for_all_rows(lambda r: write_row(t, s, r).start())

        @pl.when(is_last)
        def _():
            @pl.when(nr_ref[nt - 2] > 0)
            def _():
                wait_write(1 - s)

            @pl.when(valid)
            def _():
                wait_write(s)
                wait_fetch(1 - s)


def _moe_gmm(tile_expert, tile_rows, token_of_slot, dest, h, w_gate, w_up_lo, w_up_hi, w_down):
    n, d = h.shape
    tm, fc = MOE_TM, MOE_FC
    n_tiles = tile_rows.shape[0]
    dff = w_gate.shape[2]
    ne_lo = w_up_lo.shape[0]
    nfc = dff // fc
    rows_per_step = tm // nfc
    assert rows_per_step * nfc == tm and tm % ROW_COPY_UNROLL == 0 and n_tiles >= 2
    assert tm % 48 == 0
    assert token_of_slot.shape[0] == dest.shape[0] == (n_tiles + 1) * tm

    def chunk(t, j, nr):
        return jnp.where(nr[t] > 0, j, nfc - 1)

    def up_lo_block(t, j, te, nr, tk, ds):
        lo = te[t] < ne_lo
        return (jnp.minimum(te[t], ne_lo - 1), 0, jnp.where(lo, chunk(t, j, nr), nfc - 1))

    def up_hi_block(t, j, te, nr, tk, ds):
        hi = te[t] >= ne_lo
        return (jnp.maximum(te[t] - ne_lo, 0), 0, jnp.where(hi, chunk(t, j, nr), 0))

    grid_spec = pltpu.PrefetchScalarGridSpec(
        num_scalar_prefetch=4,
        grid=(n_tiles, nfc),
        in_specs=[
            pl.BlockSpec(memory_space=pl.ANY),
            pl.BlockSpec((None, d, fc), lambda t, j, te, nr, tk, ds: (te[t], 0, chunk(t, j, nr))),
            pl.BlockSpec((None, d, fc), up_lo_block),
            pl.BlockSpec((None, d, fc), up_hi_block),
            pl.BlockSpec((None, fc, d), lambda t, j, te, nr, tk, ds: (te[t], chunk(t, j, nr), 0)),
        ],
        out_specs=pl.BlockSpec(memory_space=pl.ANY),
        scratch_shapes=[
            pltpu.VMEM((2, tm, d), F32),
            pltpu.VMEM((tm, d), BF16),
            pltpu.VMEM((2, tm, d), F32),
            pltpu.SemaphoreType.DMA((2,)),
            pltpu.SemaphoreType.DMA((2,)),
        ],
    )
    return pl.pallas_call(
        functools.partial(_moe_kernel, ne_lo=ne_lo, rows_per_step=rows_per_step),
        grid_spec=grid_spec,
        out_shape=jax.ShapeDtypeStruct((TOP_K * n + 2 * tm, d), F32),
        compiler_params=_params(("arbitrary", "arbitrary")),
        name="moe_gmm",
    )(tile_expert, tile_rows, token_of_slot, dest, h, w_gate, w_up_lo, w_up_hi, w_down)


def _moe_plan(idx, ranks, counts, n_tokens):
    tm = MOE_TM
    n_assign = TOP_K * n_tokens
    n_tiles = -(-n_assign // tm) + N_EXPERTS
    n_slots = n_tiles * tm
    padded = ((counts + tm - 1) // tm) * tm
    ends = jnp.cumsum(padded)
    offsets = ends - padded
    experts = jnp.arange(N_EXPERTS, dtype=jnp.int32)
    slot = (jnp.sum(jnp.where(idx[..., None] == experts, offsets, 0), axis=-1) + ranks).reshape(-1)
    assign_of_slot = jnp.full((n_slots,), -1, jnp.int32).at[slot].set(jnp.arange(n_assign, dtype=jnp.int32))
    is_pad = assign_of_slot < 0
    token_of_slot = jnp.where(is_pad, 0, assign_of_slot % n_tokens)
    token_of_slot = jnp.concatenate([token_of_slot, jnp.zeros((tm,), jnp.int32)])
    p = jnp.arange(-tm, n_slots, dtype=jnp.int32)
    tail_row = n_assign + ((p // tm) % 2) * tm + p % tm
    dest = jnp.where(jnp.concatenate([jnp.ones((tm,), bool), is_pad]), tail_row,
                     jnp.concatenate([jnp.zeros((tm,), jnp.int32), assign_of_slot]))
    tile_start = jnp.arange(n_tiles, dtype=jnp.int32) * tm
    tile_used = tile_start < ends[-1]
    tile_expert = jnp.sum((tile_start[:, None] >= ends[None, :]).astype(jnp.int32), axis=1)
    last_expert = jnp.sum((ends[-1] - 1 >= ends).astype(jnp.int32))
    tile_expert = jnp.where(tile_used, tile_expert, last_expert).astype(jnp.int32)
    group_end = jnp.sum(jnp.where(tile_expert[:, None] == experts, offsets + counts, 0), axis=1)
    tile_rows = jnp.where(tile_used, jnp.clip(group_end - tile_start, 0, tm), 0).astype(jnp.int32)
    return token_of_slot, dest, tile_expert, tile_rows


def _combine_kernel(x_ref, y1_ref, y2_ref, w_ref, gf_ref, o_ref, *, final_norm):
    w = w_ref[...]
    out = x_ref[...] + w[:, 0:1] * y1_ref[...] + w[:, 1:2] * y2_ref[...]
    if final_norm:
        out = _rms_rows(out, gf_ref[...])
    o_ref[...] = out


def _combine(x, yg, wts_t, g_final, final_norm):
    n, d = x.shape
    tm = CMB_TM
    nb = n // tm
    row = pl.BlockSpec((tm, d), lambda i: (i, 0))
    return pl.pallas_call(
        functools.partial(_combine_kernel, final_norm=final_norm),
        grid=(nb,),
        in_specs=[row, row, pl.BlockSpec((tm, d), lambda i: (i + nb, 0)),
                  pl.BlockSpec((tm, TOP_K), lambda i: (i, 0)), pl.BlockSpec((1, d), lambda i: (0, 0))],
        out_specs=row,
        out_shape=jax.ShapeDtypeStruct((n, d), F32),
        compiler_params=_params(("arbitrary",)),
        name="moe_combine",
    )(x, yg, yg, wts_t, g_final)


def _moe_layer(x, g, w_router, w_gate, w_up_lo, w_up_hi, w_down, g_final, final_norm):
    n, d = x.shape
    h, idx, wts, ranks, counts = _router(x, g, w_router.T)
    token_of_slot, dest, tile_expert, tile_rows = _moe_plan(idx, ranks, counts[:, 0], n)
    yg = _moe_gmm(tile_expert, tile_rows, token_of_slot, dest, h, w_gate, w_up_lo, w_up_hi, w_down)
    return _combine(x, yg, wts.T, g_final, final_norm)


def kernel(x, norm_mix, w_in, pool_w, pool_scale, w_pool_up, w_attn_up, w_out, norm_ffn, ffn_gate,
           ffn_up, ffn_down, w_router, moe_gate, moe_up, moe_down, norm_final):
    batch, seq, d = x.shape
    depth = norm_mix.shape[0]
    pool_width = pool_scale.shape[1]
    sb_width = w_attn_up.shape[1]
    n = batch * seq
    xf = x.reshape(n, d)
    g_final = norm_final.reshape(1, d)

    pool_w_b, w_pool_up_b, w_attn_up_b, w_out_b, ffn_gate_b, ffn_up_b, ffn_down_b = (
        a.astype(BF16) for a in (pool_w, w_pool_up, w_attn_up, w_out, ffn_gate, ffn_up, ffn_down))
    pool_scale_r = pool_scale.reshape(depth, 1, pool_width)
    n_exp, _, dff = moe_gate.shape[1:]
    ne_lo = n_exp // 2
    moe_gate2, moe_up2, moe_down2 = (a.reshape(-1, a.shape[-1]) for a in (moe_gate, moe_up, moe_down))

    moe_w = None
    for i in range(depth):
        last = i == depth - 1
        j = i // 2
        u, qkv, gates = _in_proj(xf, norm_mix[i].reshape(1, d), w_in, i, pool_width, sb_width)
        if 2 * j + 1 >= depth:
            cast = ()
        elif i % 2 == 0:
            cast = ((moe_gate2, j * n_exp * d, n_exp * d), (moe_up2, j * n_exp * d, ne_lo * d))
        else:
            cast = ((moe_down2, j * n_exp * dff, n_exp * dff),
                    (moe_up2, (j * n_exp + ne_lo) * d, (n_exp - ne_lo) * d))
        y_attn, cast_b = _attention(qkv, batch, seq, cast)
        xf = _mix_out(u, y_attn, gates, xf, pool_w_b, pool_scale_r, w_pool_up_b, w_attn_up_b, w_out_b, i, seq)
        gf = norm_ffn[i].reshape(1, d)
        if i % 2 == 0:
            if cast_b:
                moe_w = (cast_b[0].reshape(n_exp, d, dff), cast_b[1].reshape(ne_lo, d, dff))
            xf = _ffn(xf, gf, ffn_gate_b, ffn_up_b, ffn_down_b, j, g_final, last)
        else:
            w_gate_b, w_up_lo_b = moe_w
            w_down_b, w_up_hi_b = cast_b[0].reshape(n_exp, dff, d), cast_b[1].reshape(n_exp - ne_lo, d, dff)
            xf = _moe_layer(xf, gf, w_router[j], w_gate_b, w_up_lo_b, w_up_hi_b, w_down_b, g_final, last)
    return xf.reshape(batch, seq, d)
```
